```python
import jax, jax.numpy as jnp
from jax import lax
import numpy as np

D_MODEL = 1024
BATCH = 4
SEQ = 8192
DEPTH = 1

D_FF = 2816
D_CONV = D_MODEL // 2
D_POOL = D_MODEL - D_CONV
CONV_WIDTH = 31
POOL_WINDOWS = (2, 4, 8, 16)
N_POOL_GROUPS = len(POOL_WINDOWS)
POOL_GROUP = D_POOL // N_POOL_GROUPS
D_IN = 2 * D_CONV + D_POOL
RMS_EPS = 1e-6
LN_EPS = 1e-5
FFN_RES_WEIGHT = 0.5

kernel_name = "macaron_conv_pool_hybrid_layer"


def _rmsnorm(x, g):
    xf = x.astype(jnp.float32)
    r = lax.rsqrt(jnp.mean(xf * xf, axis=-1, keepdims=True) + RMS_EPS)
    return (xf * r).astype(x.dtype) * g


def _layernorm(x, g, b):
    xf = x.astype(jnp.float32)
    mu = jnp.mean(xf, axis=-1, keepdims=True)
    var = jnp.mean(jnp.square(xf - mu), axis=-1, keepdims=True)
    return ((xf - mu) * lax.rsqrt(var + LN_EPS)).astype(x.dtype) * g + b


def _swiglu(h, w_gate, w_up, w_down):
    return (jax.nn.silu(h @ w_gate) * (h @ w_up)) @ w_down


def _causal_depthwise_conv(u, w, b):
    c = u.shape[-1]
    rhs = w.reshape(CONV_WIDTH, 1, c).astype(u.dtype)
    out = lax.conv_general_dilated(
        u, rhs, window_strides=(1,), padding=[(CONV_WIDTH - 1, 0)],
        dimension_numbers=("NWC", "WIO", "NWC"), feature_group_count=c)
    return out + b


def _multiscale_pool(p, pool_w, pool_scale):
    bsz, t, _ = p.shape
    pf = p.reshape(bsz, t, N_POOL_GROUPS, POOL_GROUP).astype(jnp.float32)
    cs = jnp.cumsum(pf, axis=1)
    pos = jnp.arange(t, dtype=jnp.float32)[None, :, None] + 1.0
    pooled = []
    for gi, w in enumerate(POOL_WINDOWS):
        csg = cs[:, :, gi]
        lag = jnp.pad(csg, ((0, 0), (w, 0), (0, 0)))[:, :t]
        cnt = jnp.minimum(pos, float(w))
        pooled.append((csg - lag) / cnt)
    mixed = (jnp.stack(pooled, axis=2) - pf).astype(p.dtype)
    out = jnp.einsum("btgc,gcd->btgd", mixed, pool_w)
    return out.reshape(bsz, t, D_POOL) * pool_scale


def setup_inputs(seed: int = 0) -> dict:
    key = jax.random.key(seed)
    ks = jax.random.split(key, 24)
    f32 = jnp.float32

    def nrm(k, shape, fan_in):
        return jax.random.normal(k, shape, f32) * (fan_in ** -0.5)

    def gain(k, n):
        return 1.0 + 0.02 * jax.random.normal(k, (n,), f32)

    def bias(k, n):
        return 0.02 * jax.random.normal(k, (n,), f32)

    return {
        "x": jax.random.normal(ks[0], (BATCH, SEQ, D_MODEL), f32),
        "ffn1_norm": gain(ks[1], D_MODEL),
        "ffn1_w_gate": nrm(ks[2], (D_MODEL, D_FF), D_MODEL),
        "ffn1_w_up": nrm(ks[3], (D_MODEL, D_FF), D_MODEL),
        "ffn1_w_down": nrm(ks[4], (D_FF, D_MODEL), D_FF),
        "mix_norm": gain(ks[5], D_MODEL),
        "w_in": nrm(ks[6], (D_MODEL, D_IN), D_MODEL),
        "conv_dw": nrm(ks[7], (CONV_WIDTH, D_CONV), CONV_WIDTH),
        "conv_dw_b": bias(ks[8], D_CONV),
        "conv_ln_g": gain(ks[9], D_CONV),
        "conv_ln_b": bias(ks[10], D_CONV),
        "conv_pw": nrm(ks[11], (D_CONV, D_CONV), D_CONV),
        "pool_w": nrm(ks[12], (N_POOL_GROUPS, POOL_GROUP, POOL_GROUP), POOL_GROUP),
        "pool_scale": gain(ks[13], D_POOL),
        "w_out": nrm(ks[14], (D_CONV + D_POOL, D_MODEL), D_CONV + D_POOL),
        "ffn2_norm": gain(ks[15], D_MODEL),
        "ffn2_w_gate": nrm(ks[16], (D_MODEL, D_FF), D_MODEL),
        "ffn2_w_up": nrm(ks[17], (D_MODEL, D_FF), D_MODEL),
        "ffn2_w_down": nrm(ks[18], (D_FF, D_MODEL), D_FF),
        "final_norm": gain(ks[19], D_MODEL),
    }


def reference(x, ffn1_norm, ffn1_w_gate, ffn1_w_up, ffn1_w_down, mix_norm, w_in,
              conv_dw, conv_dw_b, conv_ln_g, conv_ln_b, conv_pw, pool_w, pool_scale,
              w_out, ffn2_norm, ffn2_w_gate, ffn2_w_up, ffn2_w_down, final_norm):
    for _ in range(DEPTH):
        x = x + FFN_RES_WEIGHT * _swiglu(_rmsnorm(x, ffn1_norm), ffn1_w_gate, ffn1_w_up, ffn1_w_down)

        h = _rmsnorm(x, mix_norm)
        proj = h @ w_in
        a = proj[..., :D_CONV]
        g = proj[..., D_CONV:2 * D_CONV]
        p = proj[..., 2 * D_CONV:]

        u = a * jax.nn.sigmoid(g)
        u = _causal_depthwise_conv(u, conv_dw, conv_dw_b)
        u = jax.nn.silu(_layernorm(u, conv_ln_g, conv_ln_b))
        conv_out = u @ conv_pw

        pool_out = _multiscale_pool(p, pool_w, pool_scale)

        x = x + jnp.concatenate([conv_out, pool_out], axis=-1) @ w_out

        x = x + FFN_RES_WEIGHT * _swiglu(_rmsnorm(x, ffn2_norm), ffn2_w_gate, ffn2_w_up, ffn2_w_down)
    return _rmsnorm(x, final_norm)
```

```python
import functools

import jax
import jax.numpy as jnp
from jax import lax
from jax.experimental import pallas as pl
from jax.experimental.pallas import tpu as pltpu

D_MODEL = 1024
D_FF = 2816
D_CONV = 512
D_POOL = 512
CONV_WIDTH = 31
POOL_WINDOWS = (2, 4, 8, 16)
POOL_GROUP = D_POOL // len(POOL_WINDOWS)
D_IN = 2 * D_CONV + D_POOL
RMS_EPS = 1e-6
LN_EPS = 1e-5
FFN_RES_WEIGHT = 0.5

V7X_VMEM_LIMIT_BYTES = 56 * 1024 * 1024
SUBLANES = 8

FFN_TILE = 512
MIX_TILE = 512
CONV_HALO = 32
POOL_HALO = 16
CONV_ROWS = 32

BF16 = jnp.bfloat16
F32 = jnp.float32


def _rmsnorm(x, g):
    r = lax.rsqrt(jnp.mean(x * x, axis=-1, keepdims=True) + RMS_EPS)
    return (x * r) * g


def _ffn_kernel(x_ref, ng_ref, wg_ref, wu_ref, wd_ref, fg_ref, o_ref, *, final_norm):
    x = x_ref[...]
    h = _rmsnorm(x, ng_ref[...]).astype(BF16)
    gate = jnp.dot(h, wg_ref[...], preferred_element_type=F32)
    up = jnp.dot(h, wu_ref[...], preferred_element_type=F32)
    hid = (gate * jax.nn.sigmoid(gate) * up).astype(BF16)
    y = x + FFN_RES_WEIGHT * jnp.dot(hid, wd_ref[...], preferred_element_type=F32)
    if final_norm:
        y = _rmsnorm(y, fg_ref[...])
    o_ref[...] = y


def _ffn_call(x2d, norm_g, w_gate, w_up, w_down, final_g, *, final_norm):
    n, d = x2d.shape
    ff = w_gate.shape[1]
    tm = FFN_TILE
    resident = pl.Buffered(1)
    const = lambda i: (0, 0)
    return pl.pallas_call(
        functools.partial(_ffn_kernel, final_norm=final_norm),
        grid=(n // tm,),
        in_specs=[
            pl.BlockSpec((tm, d), lambda i: (i, 0)),
            pl.BlockSpec((1, d), const),
            pl.BlockSpec((d, ff), const, pipeline_mode=resident),
            pl.BlockSpec((d, ff), const, pipeline_mode=resident),
            pl.BlockSpec((ff, d), const, pipeline_mode=resident),
            pl.BlockSpec((1, d), const),
        ],
        out_specs=pl.BlockSpec((tm, d), lambda i: (i, 0)),
        out_shape=jax.ShapeDtypeStruct((n, d), F32),
        compiler_params=pltpu.CompilerParams(
            dimension_semantics=("arbitrary",),
            vmem_limit_bytes=V7X_VMEM_LIMIT_BYTES),
        name="ffn_final" if final_norm else "ffn",
    )(x2d, norm_g, w_gate, w_up, w_down, final_g)


def _mixer_kernel(x_ref, ng_ref, win_ref, dw_ref, dwb_ref, lng_ref, lnb_ref, pw_ref,
                  poolw_ref, pscale_ref, wout_ref, o_ref, ubuf, pbuf, cbuf):
    tm = MIX_TILE
    t = pl.program_id(1)

    @pl.when(t == 0)
    def _():
        ubuf[0:CONV_HALO, :] = jnp.zeros((CONV_HALO, D_CONV), F32)
        pbuf[0:POOL_HALO, :] = jnp.zeros((POOL_HALO, D_POOL), F32)

    x = x_ref[0]
    h = _rmsnorm(x, ng_ref[...]).astype(BF16)
    proj = jnp.dot(h, win_ref[...], preferred_element_type=F32)
    a = proj[:, :D_CONV]
    g = proj[:, D_CONV:2 * D_CONV]
    p = proj[:, 2 * D_CONV:]
    ubuf[CONV_HALO:CONV_HALO + tm, :] = a * jax.nn.sigmoid(g)
    pbuf[POOL_HALO:POOL_HALO + tm, :] = p

    for r in range(tm // CONV_ROWS):
        base = r * CONV_ROWS
        acc = jnp.broadcast_to(dwb_ref[...], (CONV_ROWS, D_CONV))
        for k in range(CONV_WIDTH):
            off = CONV_HALO - (CONV_WIDTH - 1) + k + base
            acc = acc + ubuf[off:off + CONV_ROWS, :] * dw_ref[k:k + 1, :]
        cbuf[base:base + CONV_ROWS, :] = acc

    c = cbuf[...]
    mu = jnp.mean(c, axis=-1, keepdims=True)
    cc = c - mu
    var = jnp.mean(cc * cc, axis=-1, keepdims=True)
    ln = (cc * lax.rsqrt(var + LN_EPS)) * lng_ref[...] + lnb_ref[...]
    act = (ln * jax.nn.sigmoid(ln)).astype(BF16)
    conv_out = jnp.dot(act, pw_ref[...], preferred_element_type=F32)

    pos1 = (t * tm + 1 + lax.broadcasted_iota(jnp.int32, (tm, 1), 0)).astype(F32)
    pool_parts = []
    for gi, w in enumerate(POOL_WINDOWS):
        lo, hi = gi * POOL_GROUP, (gi + 1) * POOL_GROUP
        s = pbuf[POOL_HALO:POOL_HALO + tm, lo:hi]
        for j in range(1, w):
            s = s + pbuf[POOL_HALO - j:POOL_HALO - j + tm, lo:hi]
        mixed = s / jnp.minimum(pos1, float(w)) - pbuf[POOL_HALO:POOL_HALO + tm, lo:hi]
        pool_parts.append(
            jnp.dot(mixed.astype(BF16), poolw_ref[gi], preferred_element_type=F32))
    pool_out = jnp.concatenate(pool_parts, axis=-1) * pscale_ref[...]

    both = jnp.concatenate([conv_out, pool_out], axis=-1).astype(BF16)
    o_ref[0] = x + jnp.dot(both, wout_ref[...], preferred_element_type=F32)

    ubuf[0:CONV_HALO, :] = ubuf[tm:tm + CONV_HALO, :]
    pbuf[0:POOL_HALO, :] = pbuf[tm:tm + POOL_HALO, :]


def _mixer_call(x, norm_g, w_in, conv_dw, conv_dw_b, ln_g, ln_b, conv_pw, pool_w, pool_scale,
                w_out):
    b, t, d = x.shape
    tm = MIX_TILE
    c2 = lambda bi, ti: (0, 0)
    c3 = lambda bi, ti: (0, 0, 0)
    return pl.pallas_call(
        _mixer_kernel,
        grid=(b, t // tm),
        in_specs=[
            pl.BlockSpec((1, tm, d), lambda bi, ti: (bi, ti, 0)),
            pl.BlockSpec((1, d), c2),
            pl.BlockSpec((d, D_IN), c2),
            pl.BlockSpec((CONV_WIDTH, D_CONV), c2),
            pl.BlockSpec((1, D_CONV), c2),
            pl.BlockSpec((1, D_CONV), c2),
            pl.BlockSpec((1, D_CONV), c2),
            pl.BlockSpec((D_CONV, D_CONV), c2),
            pl.BlockSpec((len(POOL_WINDOWS), POOL_GROUP, POOL_GROUP), c3),
            pl.BlockSpec((1, D_POOL), c2),
            pl.BlockSpec((d, d), c2),
        ],
        out_specs=pl.BlockSpec((1, tm, d), lambda bi, ti: (bi, ti, 0)),
        out_shape=jax.ShapeDtypeStruct((b, t, d), F32),
        scratch_shapes=[
            pltpu.VMEM((CONV_HALO + tm, D_CONV), F32),
            pltpu.VMEM((POOL_HALO + tm, D_POOL), F32),
            pltpu.VMEM((tm, D_CONV), F32),
        ],
        compiler_params=pltpu.CompilerParams(
            dimension_semantics=("arbitrary", "arbitrary"),
            vmem_limit_bytes=V7X_VMEM_LIMIT_BYTES),
        name="mixer",
    )(x, norm_g, w_in, conv_dw, conv_dw_b, ln_g, ln_b, conv_pw, pool_w, pool_scale, w_out)


def kernel(x, ffn1_norm, ffn1_w_gate, ffn1_w_up, ffn1_w_down, mix_norm, w_in, conv_dw, conv_dw_b, conv_ln_g, conv_ln_b, conv_pw, pool_w, pool_scale, w_out, ffn2_norm, ffn2_w_gate, ffn2_w_up, ffn2_w_down, final_norm):
    b, t, d = x.shape
    row = lambda v: v.reshape(1, -1)
    bf = lambda w: w.astype(BF16)

    x1 = _ffn_call(x.reshape(b * t, d), row(ffn1_norm), bf(ffn1_w_gate), bf(ffn1_w_up),
                   bf(ffn1_w_down), row(final_norm), final_norm=False)
    x2 = _mixer_call(x1.reshape(b, t, d), row(mix_norm), bf(w_in), conv_dw, row(conv_dw_b),
                     row(conv_ln_g), row(conv_ln_b), bf(conv_pw), bf(pool_w), row(pool_scale),
                     bf(w_out))
    y = _ffn_call(x2.reshape(b * t, d), row(ffn2_norm), bf(ffn2_w_gate), bf(ffn2_w_up),
                  bf(ffn2_w_down), row(final_norm), final_norm=True)
    return y.reshape(b, t, d)
```

```python
import functools

import jax
import jax.numpy as jnp
from jax import lax
from jax.experimental import pallas as pl
from jax.experimental.pallas import tpu as pltpu

D_MODEL = 1024
D_FF = 2816
D_CONV = 512
D_POOL = 512
CONV_WIDTH = 31
POOL_WINDOWS = (2, 4, 8, 16)
POOL_GROUP = D_POOL // len(POOL_WINDOWS)
D_IN = 2 * D_CONV + D_POOL
RMS_EPS = 1e-6
LN_EPS = 1e-5
FFN_RES_WEIGHT = 0.5

V7X_VMEM_LIMIT_BYTES = 56 * 1024 * 1024
SUBLANES = 8
LANES = 128

FFN_TILE = 512
MIX_TILE = 512
CONV_HALO = 32
POOL_HALO = 16
CONV_ROWS = 32
POOL_ROWS = 64

BF16 = jnp.bfloat16
F32 = jnp.float32


def _rmsnorm(x, g):
    r = lax.rsqrt(jnp.mean(x * x, axis=-1, keepdims=True) + RMS_EPS)
    return (x * r) * g


def _ffn_kernel(x_ref, ng_ref, wg_ref, wu_ref, wd_ref, fg_ref, o_ref, *, final_norm):
    x = x_ref[...]
    h = _rmsnorm(x, ng_ref[...]).astype(BF16)
    gate = jnp.dot(h, wg_ref[...], preferred_element_type=F32)
    up = jnp.dot(h, wu_ref[...], preferred_element_type=F32)
    hid = (gate * jax.nn.sigmoid(gate) * up).astype(BF16)
    y = x + FFN_RES_WEIGHT * jnp.dot(hid, wd_ref[...], preferred_element_type=F32)
    if final_norm:
        y = _rmsnorm(y, fg_ref[...])
    o_ref[...] = y


def _ffn_call(x2d, norm_g, w_gate, w_up, w_down, final_g, *, final_norm):
    n, d = x2d.shape
    ff = w_gate.shape[1]
    tm = FFN_TILE
    resident = pl.Buffered(1)
    const = lambda i: (0, 0)
    return pl.pallas_call(
        functools.partial(_ffn_kernel, final_norm=final_norm),
        grid=(n // tm,),
        in_specs=[
            pl.BlockSpec((tm, d), lambda i: (i, 0)),
            pl.BlockSpec((1, d), const),
            pl.BlockSpec((d, ff), const, pipeline_mode=resident),
            pl.BlockSpec((d, ff), const, pipeline_mode=resident),
            pl.BlockSpec((ff, d), const, pipeline_mode=resident),
            pl.BlockSpec((1, d), const),
        ],
        out_specs=pl.BlockSpec((tm, d), lambda i: (i, 0)),
        out_shape=jax.ShapeDtypeStruct((n, d), F32),
        compiler_params=pltpu.CompilerParams(
            dimension_semantics=("arbitrary",),
            vmem_limit_bytes=V7X_VMEM_LIMIT_BYTES),
        name="ffn_final" if final_norm else "ffn",
    )(x2d, norm_g, w_gate, w_up, w_down, final_g)


def _slab_rows(ref, slab, first, rows):
    return ref.at[slab // 2, pl.ds(2 * first + slab % 2, rows, stride=2), :]


def _mixer_kernel(x_ref, ng_ref, win_ref, dw_ref, dwb_ref, lng_ref, lnb_ref, pw_ref,
                  poolw_ref, pscale_ref, wout_ref, o_ref, ubuf, pbuf, cbuf, mbuf):
    tm = MIX_TILE
    t = pl.program_id(1)

    @pl.when(t == 0)
    def _():
        ubuf[:, 0:2 * CONV_HALO, :] = jnp.zeros((2, 2 * CONV_HALO, LANES), F32)
        pbuf[:, 0:2 * POOL_HALO, :] = jnp.zeros((2, 2 * POOL_HALO, LANES), F32)

    x = x_ref[0]
    h = _rmsnorm(x, ng_ref[...]).astype(BF16)
    proj = jnp.dot(h, win_ref[...], preferred_element_type=F32)
    a = proj[:, :D_CONV]
    g = proj[:, D_CONV:2 * D_CONV]
    p = proj[:, 2 * D_CONV:]
    u = a * jax.nn.sigmoid(g)
    for s in range(D_CONV // LANES):
        _slab_rows(ubuf, s, CONV_HALO, tm)[...] = u[:, s * LANES:(s + 1) * LANES]
    for s in range(D_POOL // LANES):
        _slab_rows(pbuf, s, POOL_HALO, tm)[...] = p[:, s * LANES:(s + 1) * LANES]

    for r in range(tm // CONV_ROWS):
        base = r * CONV_ROWS
        for s in range(D_CONV // LANES):
            lanes = slice(s * LANES, (s + 1) * LANES)
            acc = jnp.broadcast_to(dwb_ref[:, lanes], (CONV_ROWS, LANES))
            for k in range(CONV_WIDTH):
                first = base + CONV_HALO - (CONV_WIDTH - 1) + k
                acc = acc + _slab_rows(ubuf, s, first, CONV_ROWS)[...] * dw_ref[k:k + 1, lanes]
            cbuf[base:base + CONV_ROWS, lanes] = acc

    c = cbuf[...]
    mu = jnp.mean(c, axis=-1, keepdims=True)
    cc = c - mu
    var = jnp.mean(cc * cc, axis=-1, keepdims=True)
    ln = (cc * lax.rsqrt(var + LN_EPS)) * lng_ref[...] + lnb_ref[...]
    act = (ln * jax.nn.sigmoid(ln)).astype(BF16)
    conv_out = jnp.dot(act, pw_ref[...], preferred_element_type=F32)

    for r in range(tm // POOL_ROWS):
        base = r * POOL_ROWS
        pos1 = (t * tm + base + 1
                + lax.broadcasted_iota(jnp.int32, (POOL_ROWS, 1), 0)).astype(F32)
        for gi, w in enumerate(POOL_WINDOWS):
            cur = _slab_rows(pbuf, gi, POOL_HALO + base, POOL_ROWS)[...]
            tot = cur
            for j in range(1, w):
                tot = tot + _slab_rows(pbuf, gi, POOL_HALO + base - j, POOL_ROWS)[...]
            mixed = tot / jnp.minimum(pos1, float(w)) - cur
            mbuf[base:base + POOL_ROWS, gi * POOL_GROUP:(gi + 1) * POOL_GROUP] = (
                mixed.astype(BF16))
    pool_parts = [
        jnp.dot(mbuf[:, gi * POOL_GROUP:(gi + 1) * POOL_GROUP], poolw_ref[gi],
                preferred_element_type=F32)
        for gi in range(len(POOL_WINDOWS))]
    pool_out = jnp.concatenate(pool_parts, axis=-1) * pscale_ref[...]

    both = jnp.concatenate([conv_out, pool_out], axis=-1).astype(BF16)
    o_ref[0] = x + jnp.dot(both, wout_ref[...], preferred_element_type=F32)

    ubuf[:, 0:2 * CONV_HALO, :] = ubuf[:, 2 * tm:2 * (tm + CONV_HALO), :]
    pbuf[:, 0:2 * POOL_HALO, :] = pbuf[:, 2 * tm:2 * (tm + POOL_HALO), :]


def _mixer_call(x, norm_g, w_in, conv_dw, conv_dw_b, ln_g, ln_b, conv_pw, pool_w, pool_scale,
                w_out):
    b, t, d = x.shape
    tm = MIX_TILE
    c2 = lambda bi, ti: (0, 0)
    c3 = lambda bi, ti: (0, 0, 0)
    return pl.pallas_call(
        _mixer_kernel,
        grid=(b, t // tm),
        in_specs=[
            pl.BlockSpec((1, tm, d), lambda bi, ti: (bi, ti, 0)),
            pl.BlockSpec((1, d), c2),
            pl.BlockSpec((d, D_IN), c2),
            pl.BlockSpec((CONV_WIDTH, D_CONV), c2),
            pl.BlockSpec((1, D_CONV), c2),
            pl.BlockSpec((1, D_CONV), c2),
            pl.BlockSpec((1, D_CONV), c2),
            pl.BlockSpec((D_CONV, D_CONV), c2),
            pl.BlockSpec((len(POOL_WINDOWS), POOL_GROUP, POOL_GROUP), c3),
            pl.BlockSpec((1, D_POOL), c2),
            pl.BlockSpec((d, d), c2),
        ],
        out_specs=pl.BlockSpec((1, tm, d), lambda bi, ti: (bi, ti, 0)),
        out_shape=jax.ShapeDtypeStruct((b, t, d), F32),
        scratch_shapes=[
            pltpu.VMEM((D_CONV // (2 * LANES), 2 * (CONV_HALO + tm), LANES), F32),
            pltpu.VMEM((D_POOL // (2 * LANES), 2 * (POOL_HALO + tm), LANES), F32),
            pltpu.VMEM((tm, D_CONV), F32),
            pltpu.VMEM((tm, D_POOL), BF16),
        ],
        compiler_params=pltpu.CompilerParams(
            dimension_semantics=("arbitrary", "arbitrary"),
            vmem_limit_bytes=V7X_VMEM_LIMIT_BYTES),
        name="mixer",
    )(x, norm_g, w_in, conv_dw, conv_dw_b, ln_g, ln_b, conv_pw, pool_w, pool_scale, w_out)


def kernel(x, ffn1_norm, ffn1_w_gate, ffn1_w_up, ffn1_w_down, mix_norm, w_in, conv_dw, conv_dw_b, conv_ln_g, conv_ln_b, conv_pw, pool_w, pool_scale, w_out, ffn2_norm, ffn2_w_gate, ffn2_w_up, ffn2_w_down, final_norm):
    b, t, d = x.shape
    row = lambda v: v.reshape(1, -1)
    bf = lambda w: w.astype(BF16)

    x1 = _ffn_call(x.reshape(b * t, d), row(ffn1_norm), bf(ffn1_w_gate), bf(ffn1_w_up),
                   bf(ffn1_w_down), row(final_norm), final_norm=False)
    x2 = _mixer_call(x1.reshape(b, t, d), row(mix_norm), bf(w_in), conv_dw, row(conv_dw_b),
                     row(conv_ln_g), row(conv_ln_b), bf(conv_pw), bf(pool_w), row(pool_scale),
                     bf(w_out))
    y = _ffn_call(x2.reshape(b * t, d), row(ffn2_norm), bf(ffn2_w_gate), bf(ffn2_w_up),
                  bf(ffn2_w_down), row(final_norm), final_norm=True)
    return y.reshape(b, t, d)
```

```python
import functools

import jax
import jax.numpy as jnp
from jax import lax
from jax.experimental import pallas as pl
from jax.experimental.pallas import tpu as pltpu

D_CONV = 512
D_POOL = 512
CONV_WIDTH = 31
POOL_WINDOWS = (2, 4, 8, 16)
POOL_GROUP = D_POOL // len(POOL_WINDOWS)
D_IN = 2 * D_CONV + D_POOL
RMS_EPS = 1e-6
LN_EPS = 1e-5
FFN_RES_WEIGHT = 0.5

V7X_VMEM_LIMIT_BYTES = 56 * 1024 * 1024
SUBLANES = 8
LANES = 128

TILE = 512
CONV_HALO = 32
POOL_HALO = 16
STENCIL_ROWS = 32
MXU_COLS = 256
STENCILS_IN_GATE = 16

BF16 = jnp.bfloat16
F32 = jnp.float32


def _rmsnorm(x, g):
    r = lax.rsqrt(jnp.mean(x * x, axis=-1, keepdims=True) + RMS_EPS)
    return (x * r) * g


def _swiglu_residual(x, ng_ref, wg_ref, wu_ref, wd_ref):
    h = _rmsnorm(x, ng_ref[...]).astype(BF16)
    gate = jnp.dot(h, wg_ref[...], preferred_element_type=F32)
    up = jnp.dot(h, wu_ref[...], preferred_element_type=F32)
    hid = (gate * jax.nn.sigmoid(gate) * up).astype(BF16)
    return x + FFN_RES_WEIGHT * jnp.dot(hid, wd_ref[...], preferred_element_type=F32)


def _slab_rows(ref, slab, first, rows):
    return ref.at[slab // 2, pl.ds(2 * first + slab % 2, rows, stride=2), :]


def _ready_token(arrays):
    tiles = [a[i:i + SUBLANES, j:j + LANES]
             for a in arrays
             for i in range(0, a.shape[0], SUBLANES)
             for j in range(0, a.shape[1], LANES)]
    return functools.reduce(jnp.maximum, tiles)


def _ordered_after(x, token):
    if token is None:
        return x
    as_int = token.astype(jnp.int32)
    zero = lax.shift_right_logical(lax.shift_right_logical(as_int, jnp.int32(16)), jnp.int32(16))
    head = x[:SUBLANES, :LANES] + zero.astype(F32)
    if x.shape[1] > LANES:
        head = jnp.concatenate([head, x[:SUBLANES, LANES:]], axis=1)
    return jnp.concatenate([head, x[SUBLANES:]], axis=0)


def _stencil_chunk(r, ubuf, pbuf, act_ref, mix_ref, dw_ref, dwb_ref, lng_ref, lnb_ref,
                   first_pos, anchor):
    n_slabs = D_CONV // LANES
    base = r * STENCIL_ROWS
    conv = []
    for s in range(n_slabs):
        lanes = slice(s * LANES, (s + 1) * LANES)
        acc = _ordered_after(jnp.broadcast_to(dwb_ref[:, lanes], (STENCIL_ROWS, LANES)), anchor)
        for k in range(CONV_WIDTH):
            first = base + CONV_HALO - (CONV_WIDTH - 1) + k
            acc = acc + _slab_rows(ubuf, s, first, STENCIL_ROWS)[...] * dw_ref[k:k + 1, lanes]
        conv.append(acc)
    mu = jnp.sum(sum(conv), axis=-1, keepdims=True) * (1.0 / D_CONV)
    cen = [c - mu for c in conv]
    var = jnp.sum(sum(c * c for c in cen), axis=-1, keepdims=True) * (1.0 / D_CONV)
    inv = lax.rsqrt(var + LN_EPS)
    results = []
    for s in range(n_slabs):
        lanes = slice(s * LANES, (s + 1) * LANES)
        ln = (cen[s] * inv) * lng_ref[:, lanes] + lnb_ref[:, lanes]
        act = ln * jax.nn.sigmoid(ln)
        act_ref[base:base + STENCIL_ROWS, lanes] = act.astype(BF16)
        results.append(act)
    pos1 = (first_pos + base + 1
            + lax.broadcasted_iota(jnp.int32, (STENCIL_ROWS, 1), 0)).astype(F32)
    for gi, w in enumerate(POOL_WINDOWS):
        cur = _ordered_after(_slab_rows(pbuf, gi, POOL_HALO + base, STENCIL_ROWS)[...], anchor)
        tot = cur
        for j in range(1, w):
            tot = tot + _slab_rows(pbuf, gi, POOL_HALO + base - j, STENCIL_ROWS)[...]
        mixed = tot / jnp.minimum(pos1, float(w)) - cur
        mix_ref[base:base + STENCIL_ROWS, gi * POOL_GROUP:(gi + 1) * POOL_GROUP] = (
            mixed.astype(BF16))
        results.append(mixed)
    return _ready_token(results)


def _front_kernel(x_ref, n1_ref, wg_ref, wu_ref, wd_ref, nm_ref, win_ref, dw_ref, dwb_ref,
                  lng_ref, lnb_ref, x1_ref, act_ref, mix_ref, ubuf, pbuf, *, tiles_per_seq):
    i = pl.program_id(0)
    n_gate, n_down = wg_ref.shape[1] // MXU_COLS, wd_ref.shape[1] // MXU_COLS
    n_rest = TILE // STENCIL_ROWS - STENCILS_IN_GATE

    @pl.when(i == 0)
    def _():
        ubuf[...] = jnp.zeros(ubuf.shape, F32)
        pbuf[...] = jnp.zeros(pbuf.shape, F32)

    @pl.when(lax.rem(i, tiles_per_seq) == 1)
    def _():
        ubuf[:, 0:2 * CONV_HALO, :] = jnp.zeros((2, 2 * CONV_HALO, LANES), F32)
        pbuf[:, 0:2 * POOL_HALO, :] = jnp.zeros((2, 2 * POOL_HALO, LANES), F32)

    prev_first_pos = lax.rem(i + tiles_per_seq - 1, tiles_per_seq) * TILE
    stencil_chunk = functools.partial(
        _stencil_chunk, ubuf=ubuf, pbuf=pbuf, act_ref=act_ref, mix_ref=mix_ref, dw_ref=dw_ref,
        dwb_ref=dwb_ref, lng_ref=lng_ref, lnb_ref=lnb_ref, first_pos=prev_first_pos)

    x = x_ref[...]
    h = _rmsnorm(x, n1_ref[...]).astype(BF16)
    hid, prev = [], None
    for c in range(n_gate):
        cols = slice(c * MXU_COLS, (c + 1) * MXU_COLS)
        gate = jnp.dot(h, wg_ref[:, cols], preferred_element_type=F32)
        up = jnp.dot(h, wu_ref[:, cols], preferred_element_type=F32)
        act = gate * jax.nn.sigmoid(gate) * up
        for r in range(c * STENCILS_IN_GATE // n_gate, (c + 1) * STENCILS_IN_GATE // n_gate):
            act = _ordered_after(act, stencil_chunk(r, anchor=prev))
        hid.append(act.astype(BF16))
        prev = _ready_token([gate[:SUBLANES, :LANES]])
    hid = jnp.concatenate(hid, axis=1)
    x1 = []
    for c in range(n_down):
        cols = slice(c * MXU_COLS, (c + 1) * MXU_COLS)
        part = x[:, cols] + FFN_RES_WEIGHT * jnp.dot(
            hid, wd_ref[:, cols], preferred_element_type=F32)
        token = _ready_token([part[:SUBLANES, :LANES]])
        for r in range(STENCILS_IN_GATE + c * n_rest // n_down,
                       STENCILS_IN_GATE + (c + 1) * n_rest // n_down):
            part = _ordered_after(part, stencil_chunk(r, anchor=prev))
        x1.append(part)
        prev = token
    x1 = jnp.concatenate(x1, axis=1)
    x1_ref[...] = x1

    h = _rmsnorm(x1, nm_ref[...]).astype(BF16)
    proj = jnp.dot(h, win_ref[...], preferred_element_type=F32)
    u = proj[:, :D_CONV] * jax.nn.sigmoid(proj[:, D_CONV:2 * D_CONV])
    p = proj[:, 2 * D_CONV:]

    ubuf[:, 0:2 * CONV_HALO, :] = ubuf[:, 2 * TILE:2 * (TILE + CONV_HALO), :]
    pbuf[:, 0:2 * POOL_HALO, :] = pbuf[:, 2 * TILE:2 * (TILE + POOL_HALO), :]
    for s in range(D_CONV // LANES):
        _slab_rows(ubuf, s, CONV_HALO, TILE)[...] = u[:, s * LANES:(s + 1) * LANES]
    for s in range(D_POOL // LANES):
        _slab_rows(pbuf, s, POOL_HALO, TILE)[...] = p[:, s * LANES:(s + 1) * LANES]


def _front_call(x2d, n1, wg, wu, wd, nm, win, dw, dwb, lng, lnb, *, tiles_per_seq):
    n, d = x2d.shape
    ff = wg.shape[1]
    nt = n // TILE
    resident = pl.Buffered(1)
    const = lambda i: (0, 0)
    cur = lambda i: (jnp.minimum(i, nt - 1), 0)
    prev = lambda i: (jnp.maximum(i - 1, 0), 0)
    return pl.pallas_call(
        functools.partial(_front_kernel, tiles_per_seq=tiles_per_seq),
        grid=(nt + 1,),
        in_specs=[
            pl.BlockSpec((TILE, d), cur),
            pl.BlockSpec((1, d), const),
            pl.BlockSpec((d, ff), const, pipeline_mode=resident),
            pl.BlockSpec((d, ff), const, pipeline_mode=resident),
            pl.BlockSpec((ff, d), const, pipeline_mode=resident),
            pl.BlockSpec((1, d), const),
            pl.BlockSpec((d, D_IN), const, pipeline_mode=resident),
            pl.BlockSpec((CONV_WIDTH, D_CONV), const),
            pl.BlockSpec((1, D_CONV), const),
            pl.BlockSpec((1, D_CONV), const),
            pl.BlockSpec((1, D_CONV), const),
        ],
        out_specs=[
            pl.BlockSpec((TILE, d), cur),
            pl.BlockSpec((TILE, D_CONV), prev),
            pl.BlockSpec((TILE, D_POOL), prev),
        ],
        out_shape=[
            jax.ShapeDtypeStruct((n, d), F32),
            jax.ShapeDtypeStruct((n, D_CONV), BF16),
            jax.ShapeDtypeStruct((n, D_POOL), BF16),
        ],
        scratch_shapes=[
            pltpu.VMEM((D_CONV // (2 * LANES), 2 * (CONV_HALO + TILE), LANES), F32),
            pltpu.VMEM((D_POOL // (2 * LANES), 2 * (POOL_HALO + TILE), LANES), F32),
        ],
        compiler_params=pltpu.CompilerParams(
            dimension_semantics=("arbitrary",),
            vmem_limit_bytes=V7X_VMEM_LIMIT_BYTES),
        name="front",
    )(x2d, n1, wg, wu, wd, nm, win, dw, dwb, lng, lnb)


def _back_kernel(x1_ref, act_ref, mix_ref, pw_ref, poolw_ref, pscale_ref, wout_ref, n2_ref,
                 wg_ref, wu_ref, wd_ref, nf_ref, y_ref):
    conv_out = jnp.dot(act_ref[...], pw_ref[...], preferred_element_type=F32)
    pool_parts = [
        jnp.dot(mix_ref[:, gi * POOL_GROUP:(gi + 1) * POOL_GROUP], poolw_ref[gi],
                preferred_element_type=F32)
        for gi in range(len(POOL_WINDOWS))]
    pool_out = jnp.concatenate(pool_parts, axis=-1) * pscale_ref[...]
    both = jnp.concatenate([conv_out, pool_out], axis=-1).astype(BF16)
    x2 = x1_ref[...] + jnp.dot(both, wout_ref[...], preferred_element_type=F32)
    y = _swiglu_residual(x2, n2_ref, wg_ref, wu_ref, wd_ref)
    y_ref[...] = _rmsnorm(y, nf_ref[...])


def _back_call(x1, act, mix, pw, poolw, pscale, wout, n2, wg, wu, wd, nf):
    n, d = x1.shape
    ff = wg.shape[1]
    resident = pl.Buffered(1)
    const = lambda i: (0, 0)
    tile = lambda i: (i, 0)
    return pl.pallas_call(
        _back_kernel,
        grid=(n // TILE,),
        in_specs=[
            pl.BlockSpec((TILE, d), tile),
            pl.BlockSpec((TILE, D_CONV), tile),
            pl.BlockSpec((TILE, D_POOL), tile),
            pl.BlockSpec((D_CONV, D_CONV), const, pipeline_mode=resident),
            pl.BlockSpec((len(POOL_WINDOWS), POOL_GROUP, POOL_GROUP), lambda i: (0, 0, 0)),
            pl.BlockSpec((1, D_POOL), const),
            pl.BlockSpec((d, d), const, pipeline_mode=resident),
            pl.BlockSpec((1, d), const),
            pl.BlockSpec((d, ff), const, pipeline_mode=resident),
            pl.BlockSpec((d, ff), const, pipeline_mode=resident),
            pl.BlockSpec((ff, d), const, pipeline_mode=resident),
            pl.BlockSpec((1, d), const),
        ],
        out_specs=pl.BlockSpec((TILE, d), tile),
        out_shape=jax.ShapeDtypeStruct((n, d), F32),
        compiler_params=pltpu.CompilerParams(
            dimension_semantics=("arbitrary",),
            vmem_limit_bytes=V7X_VMEM_LIMIT_BYTES),
        name="back",
    )(x1, act, mix, pw, poolw, pscale, wout, n2, wg, wu, wd, nf)


def kernel(x, ffn1_norm, ffn1_w_gate, ffn1_w_up, ffn1_w_down, mix_norm, w_in, conv_dw, conv_dw_b, conv_ln_g, conv_ln_b, conv_pw, pool_w, pool_scale, w_out, ffn2_norm, ffn2_w_gate, ffn2_w_up, ffn2_w_down, final_norm):
    b, t, d = x.shape
    assert t % TILE == 0 and t // TILE >= 2
    row = lambda v: v.reshape(1, -1)
    bf = lambda w: w.astype(BF16)

    x1, act, mix = _front_call(
        x.reshape(b * t, d), row(ffn1_norm), bf(ffn1_w_gate), bf(ffn1_w_up), bf(ffn1_w_down),
        row(mix_norm), bf(w_in), conv_dw, row(conv_dw_b), row(conv_ln_g), row(conv_ln_b),
        tiles_per_seq=t // TILE)
    y = _back_call(
        x1, act, mix, bf(conv_pw), bf(pool_w), row(pool_scale), bf(w_out), row(ffn2_norm),
        bf(ffn2_w_gate), bf(ffn2_w_up), bf(ffn2_w_down), row(final_norm))
    return y.reshape(b, t, d)
```

```python
import functools

import jax
import jax.numpy as jnp
from jax import lax
from jax.experimental import pallas as pl
from jax.experimental.pallas import tpu as pltpu

D_CONV = 512
D_POOL = 512
CONV_WIDTH = 31
POOL_WINDOWS = (2, 4, 8, 16)
POOL_GROUP = D_POOL // len(POOL_WINDOWS)
D_IN = 2 * D_CONV + D_POOL
RMS_EPS = 1e-6
LN_EPS = 1e-5
FFN_RES_WEIGHT = 0.5

V7X_VMEM_LIMIT_BYTES = 60 * 1024 * 1024
SUBLANES = 8
LANES = 128

TILE = 512
BACK_TILE = 1024
CONV_HALO = 32
POOL_HALO = 16
STENCIL_ROWS = 32
MXU_COLS = 256

BF16 = jnp.bfloat16
F32 = jnp.float32


def _rmsnorm(x, g):
    r = lax.rsqrt(jnp.mean(x * x, axis=-1, keepdims=True) + RMS_EPS)
    return (x * r) * g


def _swiglu_residual(x, ng_ref, wg_ref, wu_ref, wd_ref):
    h = _rmsnorm(x, ng_ref[...]).astype(BF16)
    gate = jnp.dot(h, wg_ref[...], preferred_element_type=F32)
    up = jnp.dot(h, wu_ref[...], preferred_element_type=F32)
    hid = (gate * jax.nn.sigmoid(gate) * up).astype(BF16)
    return x + FFN_RES_WEIGHT * jnp.dot(hid, wd_ref[...], preferred_element_type=F32)


def _slab_rows(ref, slab, first, rows):
    return ref.at[slab // 2, pl.ds(2 * first + slab % 2, rows, stride=2), :]


def _ready_token(arrays):
    tiles = [a[i:i + SUBLANES, j:j + LANES]
             for a in arrays
             for i in range(0, a.shape[0], SUBLANES)
             for j in range(0, a.shape[1], LANES)]
    return functools.reduce(jnp.maximum, tiles)


def _ordered_after(x, token, every_tile=False):
    if token is None:
        return x
    as_int = token.astype(jnp.int32)
    zero = lax.shift_right_logical(lax.shift_right_logical(as_int, jnp.int32(16)), jnp.int32(16))
    zero = zero.astype(F32)
    if every_tile:
        return x + jnp.concatenate([zero] * (x.shape[0] // SUBLANES), axis=0)
    head = x[:SUBLANES, :LANES] + zero
    if x.shape[1] > LANES:
        head = jnp.concatenate([head, x[:SUBLANES, LANES:]], axis=1)
    return jnp.concatenate([head, x[SUBLANES:]], axis=0)


def _stencil_chunk(r, ubuf, pbuf, act_ref, mix_ref, dw_ref, dwb_ref, lng_ref, lnb_ref,
                   first_pos, anchor):
    n_slabs = D_CONV // LANES
    base = r * STENCIL_ROWS
    conv, conv_done = [], anchor
    for s in range(n_slabs):
        lanes = slice(s * LANES, (s + 1) * LANES)
        acc = _ordered_after(jnp.broadcast_to(dwb_ref[:, lanes], (STENCIL_ROWS, LANES)),
                             conv_done, every_tile=True)
        for k in range(CONV_WIDTH):
            first = base + CONV_HALO - (CONV_WIDTH - 1) + k
            acc = acc + _slab_rows(ubuf, s, first, STENCIL_ROWS)[...] * dw_ref[k:k + 1, lanes]
        conv.append(acc)
        conv_done = _ready_token([acc])
    mu = jnp.sum(sum(conv), axis=-1, keepdims=True) * (1.0 / D_CONV)
    cen = [c - mu for c in conv]
    var = jnp.sum(sum(c * c for c in cen), axis=-1, keepdims=True) * (1.0 / D_CONV)
    inv = lax.rsqrt(var + LN_EPS)
    results = []
    for s in range(n_slabs):
        lanes = slice(s * LANES, (s + 1) * LANES)
        ln = (cen[s] * inv) * lng_ref[:, lanes] + lnb_ref[:, lanes]
        act = ln * jax.nn.sigmoid(ln)
        act_ref[base:base + STENCIL_ROWS, lanes] = act.astype(BF16)
        results.append(act)
    pos1 = (first_pos + base + 1
            + lax.broadcasted_iota(jnp.int32, (STENCIL_ROWS, 1), 0)).astype(F32)
    for gi, w in enumerate(POOL_WINDOWS):
        cur = _ordered_after(_slab_rows(pbuf, gi, POOL_HALO + base, STENCIL_ROWS)[...], anchor)
        tot = cur
        for j in range(1, w):
            tot = tot + _slab_rows(pbuf, gi, POOL_HALO + base - j, STENCIL_ROWS)[...]
        mixed = tot / jnp.minimum(pos1, float(w)) - cur
        mix_ref[base:base + STENCIL_ROWS, gi * POOL_GROUP:(gi + 1) * POOL_GROUP] = (
            mixed.astype(BF16))
        results.append(mixed)
    return _ready_token(results), conv_done


def _front_kernel(x_ref, n1_ref, wg_ref, wu_ref, wd_ref, nm_ref, win_ref, dw_ref, dwb_ref,
                  lng_ref, lnb_ref, x1_ref, act_ref, mix_ref, ubuf, pbuf, *, tiles_per_seq):
    i = pl.program_id(0)
    n_gate, n_stencil = wg_ref.shape[1] // MXU_COLS, TILE // STENCIL_ROWS

    @pl.when(i == 0)
    def _():
        ubuf[...] = jnp.zeros(ubuf.shape, F32)
        pbuf[...] = jnp.zeros(pbuf.shape, F32)

    @pl.when(lax.rem(i, tiles_per_seq) == 1)
    def _():
        ubuf[:, 0:2 * CONV_HALO, :] = jnp.zeros((2, 2 * CONV_HALO, LANES), F32)
        pbuf[:, 0:2 * POOL_HALO, :] = jnp.zeros((2, 2 * POOL_HALO, LANES), F32)

    prev_first_pos = lax.rem(i + tiles_per_seq - 1, tiles_per_seq) * TILE
    stencil_chunk = functools.partial(
        _stencil_chunk, ubuf=ubuf, pbuf=pbuf, act_ref=act_ref, mix_ref=mix_ref, dw_ref=dw_ref,
        dwb_ref=dwb_ref, lng_ref=lng_ref, lnb_ref=lnb_ref, first_pos=prev_first_pos)

    x = x_ref[...]
    h = _rmsnorm(x, n1_ref[...]).astype(BF16)
    hid, prev = [], None
    for c in range(n_gate):
        cols = slice(c * MXU_COLS, (c + 1) * MXU_COLS)
        gate = jnp.dot(h, wg_ref[:, cols], preferred_element_type=F32)
        up = jnp.dot(h, wu_ref[:, cols], preferred_element_type=F32)
        act = gate * jax.nn.sigmoid(gate) * up
        for r in range(c * n_stencil // n_gate, (c + 1) * n_stencil // n_gate):
            done, prev = stencil_chunk(r, anchor=prev)
            act = _ordered_after(act, done)
        hid.append(act.astype(BF16))
        prev = _ready_token([gate[:SUBLANES, :LANES]] + ([] if prev is None else [prev]))
    hid = jnp.concatenate(hid, axis=1)
    x1 = x + FFN_RES_WEIGHT * jnp.dot(hid, wd_ref[...], preferred_element_type=F32)
    x1_ref[...] = x1

    h = _rmsnorm(x1, nm_ref[...]).astype(BF16)
    proj = jnp.dot(h, win_ref[...], preferred_element_type=F32)
    u = proj[:, :D_CONV] * jax.nn.sigmoid(proj[:, D_CONV:2 * D_CONV])
    p = proj[:, 2 * D_CONV:]

    ubuf[:, 0:2 * CONV_HALO, :] = ubuf[:, 2 * TILE:2 * (TILE + CONV_HALO), :]
    pbuf[:, 0:2 * POOL_HALO, :] = pbuf[:, 2 * TILE:2 * (TILE + POOL_HALO), :]
    for s in range(D_CONV // LANES):
        _slab_rows(ubuf, s, CONV_HALO, TILE)[...] = u[:, s * LANES:(s + 1) * LANES]
    for s in range(D_POOL // LANES):
        _slab_rows(pbuf, s, POOL_HALO, TILE)[...] = p[:, s * LANES:(s + 1) * LANES]


def _front_call(x2d, n1, wg, wu, wd, nm, win, dw, dwb, lng, lnb, *, tiles_per_seq):
    n, d = x2d.shape
    ff = wg.shape[1]
    nt = n // TILE
    resident = pl.Buffered(1)
    const = lambda i: (0, 0)
    cur = lambda i: (jnp.minimum(i, nt - 1), 0)
    prev = lambda i: (jnp.maximum(i - 1, 0), 0)
    return pl.pallas_call(
        functools.partial(_front_kernel, tiles_per_seq=tiles_per_seq),
        grid=(nt + 1,),
        in_specs=[
            pl.BlockSpec((TILE, d), cur),
            pl.BlockSpec((1, d), const),
            pl.BlockSpec((d, ff), const, pipeline_mode=resident),
            pl.BlockSpec((d, ff), const, pipeline_mode=resident),
            pl.BlockSpec((ff, d), const, pipeline_mode=resident),
            pl.BlockSpec((1, d), const),
            pl.BlockSpec((d, D_IN), const, pipeline_mode=resident),
            pl.BlockSpec((CONV_WIDTH, D_CONV), const),
            pl.BlockSpec((1, D_CONV), const),
            pl.BlockSpec((1, D_CONV), const),
            pl.BlockSpec((1, D_CONV), const),
        ],
        out_specs=[
            pl.BlockSpec((TILE, d), cur),
            pl.BlockSpec((TILE, D_CONV), prev),
            pl.BlockSpec((TILE, D_POOL), prev),
        ],
        out_shape=[
            jax.ShapeDtypeStruct((n, d), F32),
            jax.ShapeDtypeStruct((n, D_CONV), BF16),
            jax.ShapeDtypeStruct((n, D_POOL), BF16),
        ],
        scratch_shapes=[
            pltpu.VMEM((D_CONV // (2 * LANES), 2 * (CONV_HALO + TILE), LANES), F32),
            pltpu.VMEM((D_POOL // (2 * LANES), 2 * (POOL_HALO + TILE), LANES), F32),
        ],
        compiler_params=pltpu.CompilerParams(
            dimension_semantics=("arbitrary",),
            vmem_limit_bytes=V7X_VMEM_LIMIT_BYTES),
        name="front",
    )(x2d, n1, wg, wu, wd, nm, win, dw, dwb, lng, lnb)


def _back_kernel(x1_ref, act_ref, mix_ref, pw_ref, poolw_ref, pscale_ref, wout_ref, n2_ref,
                 wg_ref, wu_ref, wd_ref, nf_ref, y_ref):
    conv_out = jnp.dot(act_ref[...], pw_ref[...], preferred_element_type=F32)
    pool_parts = [
        jnp.dot(mix_ref[:, gi * POOL_GROUP:(gi + 1) * POOL_GROUP], poolw_ref[gi],
                preferred_element_type=F32)
        for gi in range(len(POOL_WINDOWS))]
    pool_out = jnp.concatenate(pool_parts, axis=-1) * pscale_ref[...]
    both = jnp.concatenate([conv_out, pool_out], axis=-1).astype(BF16)
    x2 = x1_ref[...] + jnp.dot(both, wout_ref[...], preferred_element_type=F32)
    y = _swiglu_residual(x2, n2_ref, wg_ref, wu_ref, wd_ref)
    y_ref[...] = _rmsnorm(y, nf_ref[...])


def _back_call(x1, act, mix, pw, poolw, pscale, wout, n2, wg, wu, wd, nf):
    n, d = x1.shape
    ff = wg.shape[1]
    resident = pl.Buffered(1)
    const = lambda i: (0, 0)
    tile = lambda i: (i, 0)
    return pl.pallas_call(
        _back_kernel,
        grid=(n // BACK_TILE,),
        in_specs=[
            pl.BlockSpec((BACK_TILE, d), tile),
            pl.BlockSpec((BACK_TILE, D_CONV), tile),
            pl.BlockSpec((BACK_TILE, D_POOL), tile),
            pl.BlockSpec((D_CONV, D_CONV), const, pipeline_mode=resident),
            pl.BlockSpec((len(POOL_WINDOWS), POOL_GROUP, POOL_GROUP), lambda i: (0, 0, 0)),
            pl.BlockSpec((1, D_POOL), const),
            pl.BlockSpec((d, d), const, pipeline_mode=resident),
            pl.BlockSpec((1, d), const),
            pl.BlockSpec((d, ff), const, pipeline_mode=resident),
            pl.BlockSpec((d, ff), const, pipeline_mode=resident),
            pl.BlockSpec((ff, d), const, pipeline_mode=resident),
            pl.BlockSpec((1, d), const),
        ],
        out_specs=pl.BlockSpec((BACK_TILE, d), tile),
        out_shape=jax.ShapeDtypeStruct((n, d), F32),
        compiler_params=pltpu.CompilerParams(
            dimension_semantics=("arbitrary",),
            vmem_limit_bytes=V7X_VMEM_LIMIT_BYTES),
        name="back",
    )(x1, act, mix, pw, poolw, pscale, wout, n2, wg, wu, wd, nf)


def kernel(x, ffn1_norm, ffn1_w_gate, ffn1_w_up, ffn1_w_down, mix_norm, w_in, conv_dw, conv_dw_b, conv_ln_g, conv_ln_b, conv_pw, pool_w, pool_scale, w_out, ffn2_norm, ffn2_w_gate, ffn2_w_up, ffn2_w_down, final_norm):
    b, t, d = x.shape
    assert t % TILE == 0 and t // TILE >= 2
    row = lambda v: v.reshape(1, -1)
    bf = lambda w: w.astype(BF16)

    x1, act, mix = _front_call(
        x.reshape(b * t, d), row(ffn1_norm), bf(ffn1_w_gate), bf(ffn1_w_up), bf(ffn1_w_down),
        row(mix_norm), bf(w_in), conv_dw, row(conv_dw_b), row(conv_ln_g), row(conv_ln_b),
        tiles_per_seq=t // TILE)
    y = _back_call(
        x1, act, mix, bf(conv_pw), bf(pool_w), row(pool_scale), bf(w_out), row(ffn2_norm),
        bf(ffn2_w_gate), bf(ffn2_w_up), bf(ffn2_w_down), row(final_norm))
    return y.reshape(b, t, d)
```

```python
import functools

import jax
import jax.numpy as jnp
from jax import lax
from jax.experimental import pallas as pl
from jax.experimental.pallas import tpu as pltpu

D_CONV = 512
D_POOL = 512
CONV_WIDTH = 31
POOL_WINDOWS = (2, 4, 8, 16)
POOL_GROUP = D_POOL // len(POOL_WINDOWS)
D_IN = 2 * D_CONV + D_POOL
RMS_EPS = 1e-6
LN_EPS = 1e-5
FFN_RES_WEIGHT = 0.5

V7X_VMEM_LIMIT_BYTES = 60 * 1024 * 1024
SUBLANES = 8
BF16_SUBLANES = 16
LANES = 128

TILE = 512
BACK_TILE = 1024
CONV_HALO = 32
POOL_HALO = 16
STENCIL_ROWS = 16
MXU_COLS = 256

BF16 = jnp.bfloat16
F32 = jnp.float32


def _rmsnorm(x, g):
    r = lax.rsqrt(jnp.mean(x * x, axis=-1, keepdims=True) + RMS_EPS)
    return (x * r) * g


def _swiglu_residual(x, ng_ref, wg_ref, wu_ref, wd_ref):
    h = _rmsnorm(x, ng_ref[...]).astype(BF16)
    gate = jnp.dot(h, wg_ref[...], preferred_element_type=F32)
    up = jnp.dot(h, wu_ref[...], preferred_element_type=F32)
    hid = (gate * jax.nn.sigmoid(gate) * up).astype(BF16)
    return x + FFN_RES_WEIGHT * jnp.dot(hid, wd_ref[...], preferred_element_type=F32)


def _slab_rows(ref, slab, first, rows):
    return ref.at[slab // 2, pl.ds(2 * first + slab % 2, rows, stride=2), :]


def _ready_token(arrays):
    tiles = [a[i:i + SUBLANES, j:j + LANES]
             for a in arrays
             for i in range(0, a.shape[0], SUBLANES)
             for j in range(0, a.shape[1], LANES)]
    return functools.reduce(jnp.maximum, tiles)


def _ordered_after(x, token, every_tile=False):
    if token is None:
        return x
    as_int = token.astype(jnp.int32)
    zero = lax.shift_right_logical(lax.shift_right_logical(as_int, jnp.int32(16)), jnp.int32(16))
    zero = zero.astype(F32)
    if every_tile:
        return x + jnp.concatenate([zero] * (x.shape[0] // SUBLANES), axis=0)
    head = x[:SUBLANES, :LANES] + zero
    if x.shape[1] > LANES:
        head = jnp.concatenate([head, x[:SUBLANES, LANES:]], axis=1)
    return jnp.concatenate([head, x[SUBLANES:]], axis=0)


def _stencil_chunk(r, ubuf, pbuf, act_ref, mix_ref, dw_ref, dwb_ref, lng_ref, lnb_ref,
                   first_pos, anchor):
    n_slabs = D_CONV // LANES
    base = r * STENCIL_ROWS
    conv, conv_done = [], anchor
    for s in range(n_slabs):
        lanes = slice(s * LANES, (s + 1) * LANES)
        acc = _ordered_after(jnp.broadcast_to(dwb_ref[:, lanes], (STENCIL_ROWS, LANES)),
                             conv_done, every_tile=True)
        for k in range(CONV_WIDTH):
            first = base + CONV_HALO - (CONV_WIDTH - 1) + k
            acc = acc + _slab_rows(ubuf, s, first, STENCIL_ROWS)[...] * dw_ref[k:k + 1, lanes]
        conv.append(acc)
        conv_done = _ready_token([acc])
    mu = jnp.sum(sum(conv), axis=-1, keepdims=True) * (1.0 / D_CONV)
    cen = [c - mu for c in conv]
    var = jnp.sum(sum(c * c for c in cen), axis=-1, keepdims=True) * (1.0 / D_CONV)
    inv = lax.rsqrt(var + LN_EPS)
    results = []
    for s in range(n_slabs):
        lanes = slice(s * LANES, (s + 1) * LANES)
        ln = (cen[s] * inv) * lng_ref[:, lanes] + lnb_ref[:, lanes]
        act = ln * jax.nn.sigmoid(ln)
        act_ref[base:base + STENCIL_ROWS, lanes] = act.astype(BF16)
        results.append(act)
    pos1 = (first_pos + base + 1
            + lax.broadcasted_iota(jnp.int32, (STENCIL_ROWS, 1), 0)).astype(F32)
    for gi, w in enumerate(POOL_WINDOWS):
        cur = _ordered_after(_slab_rows(pbuf, gi, POOL_HALO + base, STENCIL_ROWS)[...], anchor)
        tot = cur
        for j in range(1, w):
            tot = tot + _slab_rows(pbuf, gi, POOL_HALO + base - j, STENCIL_ROWS)[...]
        mixed = tot * (1.0 / jnp.minimum(pos1, float(w))) - cur
        mix_ref[base:base + STENCIL_ROWS, gi * POOL_GROUP:(gi + 1) * POOL_GROUP] = (
            mixed.astype(BF16))
        results.append(mixed)
    return _ready_token(results), conv_done


def _front_kernel(x_ref, n1_ref, wg_ref, wu_ref, wd_ref, nm_ref, win_ref, dw_ref, dwb_ref,
                  lng_ref, lnb_ref, late0_ref, late1_ref, late2_ref, late3_ref,
                  x1_ref, act_ref, mix_ref, cast0_ref, cast1_ref, cast2_ref, cast3_ref,
                  ubuf, pbuf, *, tiles_per_seq):
    i = pl.program_id(0)

    for src, dst in ((late0_ref, cast0_ref), (late1_ref, cast1_ref), (late2_ref, cast2_ref),
                     (late3_ref, cast3_ref)):
        dst[...] = src[...].astype(BF16)
    n_gate, n_stencil = wg_ref.shape[1] // MXU_COLS, TILE // STENCIL_ROWS

    @pl.when(i == 0)
    def _():
        ubuf[...] = jnp.zeros(ubuf.shape, F32)
        pbuf[...] = jnp.zeros(pbuf.shape, F32)

    @pl.when(lax.rem(i, tiles_per_seq) == 1)
    def _():
        ubuf[:, 0:2 * CONV_HALO, :] = jnp.zeros((2, 2 * CONV_HALO, LANES), F32)
        pbuf[:, 0:2 * POOL_HALO, :] = jnp.zeros((2, 2 * POOL_HALO, LANES), F32)

    prev_first_pos = lax.rem(i + tiles_per_seq - 1, tiles_per_seq) * TILE
    stencil_chunk = functools.partial(
        _stencil_chunk, ubuf=ubuf, pbuf=pbuf, act_ref=act_ref, mix_ref=mix_ref, dw_ref=dw_ref,
        dwb_ref=dwb_ref, lng_ref=lng_ref, lnb_ref=lnb_ref, first_pos=prev_first_pos)

    x = x_ref[...]
    h = _rmsnorm(x, n1_ref[...]).astype(BF16)
    hid, prev = [], None
    for c in range(n_gate):
        cols = slice(c * MXU_COLS, (c + 1) * MXU_COLS)
        gate = jnp.dot(h, wg_ref[:, cols], preferred_element_type=F32)
        up = jnp.dot(h, wu_ref[:, cols], preferred_element_type=F32)
        act = gate * jax.nn.sigmoid(gate) * up
        for r in range(c * n_stencil // n_gate, (c + 1) * n_stencil // n_gate):
            done, prev = stencil_chunk(r, anchor=prev)
            act = _ordered_after(act, done)
        hid.append(act.astype(BF16))
        prev = _ready_token([gate[:SUBLANES, :LANES]] + ([] if prev is None else [prev]))
    hid = jnp.concatenate(hid, axis=1)
    x1 = x + FFN_RES_WEIGHT * jnp.dot(hid, wd_ref[...], preferred_element_type=F32)
    x1_ref[...] = x1

    h = _rmsnorm(x1, nm_ref[...]).astype(BF16)
    proj = jnp.dot(h, win_ref[...], preferred_element_type=F32)
    u = proj[:, :D_CONV] * jax.nn.sigmoid(proj[:, D_CONV:2 * D_CONV])
    p = proj[:, 2 * D_CONV:]

    ubuf[:, 0:2 * CONV_HALO, :] = ubuf[:, 2 * TILE:2 * (TILE + CONV_HALO), :]
    pbuf[:, 0:2 * POOL_HALO, :] = pbuf[:, 2 * TILE:2 * (TILE + POOL_HALO), :]
    for s in range(D_CONV // LANES):
        _slab_rows(ubuf, s, CONV_HALO, TILE)[...] = u[:, s * LANES:(s + 1) * LANES]
    for s in range(D_POOL // LANES):
        _slab_rows(pbuf, s, POOL_HALO, TILE)[...] = p[:, s * LANES:(s + 1) * LANES]


def _front_call(x2d, n1, wg, wu, wd, nm, win, dw, dwb, lng, lnb, late_weights, *,
                tiles_per_seq):
    n, d = x2d.shape
    ff = wg.shape[1]
    nt = n // TILE
    resident = pl.Buffered(1)
    const = lambda i: (0, 0)
    cur = lambda i: (jnp.minimum(i, nt - 1), 0)
    prev = lambda i: (jnp.maximum(i - 1, 0), 0)

    def late_block(w):
        rows = next(r for r in range(BF16_SUBLANES, w.shape[0] + 1, BF16_SUBLANES)
                    if w.shape[0] % r == 0 and w.shape[0] // r <= nt)
        return pl.BlockSpec((rows, w.shape[1]),
                            lambda i: (jnp.minimum(i, w.shape[0] // rows - 1), 0))

    return pl.pallas_call(
        functools.partial(_front_kernel, tiles_per_seq=tiles_per_seq),
        grid=(nt + 1,),
        in_specs=[
            pl.BlockSpec((TILE, d), cur),
            pl.BlockSpec((1, d), const),
            pl.BlockSpec((d, ff), const, pipeline_mode=resident),
            pl.BlockSpec((d, ff), const, pipeline_mode=resident),
            pl.BlockSpec((ff, d), const, pipeline_mode=resident),
            pl.BlockSpec((1, d), const),
            pl.BlockSpec((d, D_IN), const, pipeline_mode=resident),
            pl.BlockSpec((CONV_WIDTH, D_CONV), const),
            pl.BlockSpec((1, D_CONV), const),
            pl.BlockSpec((1, D_CONV), const),
            pl.BlockSpec((1, D_CONV), const),
        ] + [late_block(w) for w in late_weights],
        out_specs=[
            pl.BlockSpec((TILE, d), cur),
            pl.BlockSpec((TILE, D_CONV), prev),
            pl.BlockSpec((TILE, D_POOL), prev),
        ] + [late_block(w) for w in late_weights],
        out_shape=[
            jax.ShapeDtypeStruct((n, d), F32),
            jax.ShapeDtypeStruct((n, D_CONV), BF16),
            jax.ShapeDtypeStruct((n, D_POOL), BF16),
        ] + [jax.ShapeDtypeStruct(w.shape, BF16) for w in late_weights],
        scratch_shapes=[
            pltpu.VMEM((D_CONV // (2 * LANES), 2 * (CONV_HALO + TILE), LANES), F32),
            pltpu.VMEM((D_POOL // (2 * LANES), 2 * (POOL_HALO + TILE), LANES), F32),
        ],
        compiler_params=pltpu.CompilerParams(
            dimension_semantics=("arbitrary",),
            vmem_limit_bytes=V7X_VMEM_LIMIT_BYTES),
        name="front",
    )(x2d, n1, wg, wu, wd, nm, win, dw, dwb, lng, lnb, *late_weights)


def _back_kernel(x1_ref, act_ref, mix_ref, pw_ref, poolw_ref, pscale_ref, wout_ref, n2_ref,
                 wg_ref, wu_ref, wd_ref, nf_ref, y_ref):
    conv_out = jnp.dot(act_ref[...], pw_ref[...], preferred_element_type=F32)
    pool_parts = [
        jnp.dot(mix_ref[:, gi * POOL_GROUP:(gi + 1) * POOL_GROUP], poolw_ref[gi],
                preferred_element_type=F32)
        for gi in range(len(POOL_WINDOWS))]
    pool_out = jnp.concatenate(pool_parts, axis=-1) * pscale_ref[...]
    both = jnp.concatenate([conv_out, pool_out], axis=-1).astype(BF16)
    x2 = x1_ref[...] + jnp.dot(both, wout_ref[...], preferred_element_type=F32)
    y = _swiglu_residual(x2, n2_ref, wg_ref, wu_ref, wd_ref)
    y_ref[...] = _rmsnorm(y, nf_ref[...])


def _back_call(x1, act, mix, pw, poolw, pscale, wout, n2, wg, wu, wd, nf):
    n, d = x1.shape
    ff = wg.shape[1]
    resident = pl.Buffered(1)
    const = lambda i: (0, 0)
    tile = lambda i: (i, 0)
    return pl.pallas_call(
        _back_kernel,
        grid=(n // BACK_TILE,),
        in_specs=[
            pl.BlockSpec((BACK_TILE, d), tile),
            pl.BlockSpec((BACK_TILE, D_CONV), tile),
            pl.BlockSpec((BACK_TILE, D_POOL), tile),
            pl.BlockSpec((D_CONV, D_CONV), const, pipeline_mode=resident),
            pl.BlockSpec((len(POOL_WINDOWS), POOL_GROUP, POOL_GROUP), lambda i: (0, 0, 0)),
            pl.BlockSpec((1, D_POOL), const),
            pl.BlockSpec((d, d), const, pipeline_mode=resident),
            pl.BlockSpec((1, d), const),
            pl.BlockSpec((d, ff), const, pipeline_mode=resident),
            pl.BlockSpec((d, ff), const, pipeline_mode=resident),
            pl.BlockSpec((ff, d), const, pipeline_mode=resident),
            pl.BlockSpec((1, d), const),
        ],
        out_specs=pl.BlockSpec((BACK_TILE, d), tile),
        out_shape=jax.ShapeDtypeStruct((n, d), F32),
        compiler_params=pltpu.CompilerParams(
            dimension_semantics=("arbitrary",),
            vmem_limit_bytes=V7X_VMEM_LIMIT_BYTES),
        name="back",
    )(x1, act, mix, pw, poolw, pscale, wout, n2, wg, wu, wd, nf)


def kernel(x, ffn1_norm, ffn1_w_gate, ffn1_w_up, ffn1_w_down, mix_norm, w_in, conv_dw, conv_dw_b, conv_ln_g, conv_ln_b, conv_pw, pool_w, pool_scale, w_out, ffn2_norm, ffn2_w_gate, ffn2_w_up, ffn2_w_down, final_norm):
    b, t, d = x.shape
    assert t % TILE == 0 and t // TILE >= 2
    row = lambda v: v.reshape(1, -1)
    bf = lambda w: w.astype(BF16)

    late = [ffn2_w_gate, ffn2_w_up, ffn2_w_down, w_out]
    x1, act, mix, wg2, wu2, wd2, wout = _front_call(
        x.reshape(b * t, d), row(ffn1_norm), bf(ffn1_w_gate), bf(ffn1_w_up), bf(ffn1_w_down),
        row(mix_norm), bf(w_in), conv_dw, row(conv_dw_b), row(conv_ln_g), row(conv_ln_b),
        late, tiles_per_seq=t // TILE)
    y = _back_call(
        x1, act, mix, bf(conv_pw), bf(pool_w), row(pool_scale), wout, row(ffn2_norm),
        wg2, wu2, wd2, row(final_norm))
    return y.reshape(b, t, d)
```

```python
import functools

import jax
import jax.numpy as jnp
from jax import lax
from jax.experimental import pallas as pl
from jax.experimental.pallas import tpu as pltpu

D_CONV = 512
D_POOL = 512
CONV_WIDTH = 31
POOL_WINDOWS = (2, 4, 8, 16)
POOL_GROUP = D_POOL // len(POOL_WINDOWS)
D_IN = 2 * D_CONV + D_POOL
RMS_EPS = 1e-6
LN_EPS = 1e-5
FFN_RES_WEIGHT = 0.5

V7X_VMEM_LIMIT_BYTES = 60 * 1024 * 1024
SUBLANES = 8
BF16_SUBLANES = 16
LANES = 128

TILE = 512
BACK_TILE = 1024
CONV_HALO = 32
POOL_HALO = 16
STENCIL_ROWS = 16
MXU_COLS = 256
STENCILS_UNDER_DOWN = 14

BF16 = jnp.bfloat16
F32 = jnp.float32


def _rmsnorm(x, g):
    r = lax.rsqrt(jnp.mean(x * x, axis=-1, keepdims=True) + RMS_EPS)
    return (x * r) * g


def _swiglu_residual(x, ng_ref, wg_ref, wu_ref, wd_ref):
    h = _rmsnorm(x, ng_ref[...]).astype(BF16)
    gate = jnp.dot(h, wg_ref[...], preferred_element_type=F32)
    up = jnp.dot(h, wu_ref[...], preferred_element_type=F32)
    hid = (gate * jax.nn.sigmoid(gate) * up).astype(BF16)
    return x + FFN_RES_WEIGHT * jnp.dot(hid, wd_ref[...], preferred_element_type=F32)


def _slab_rows(ref, slab, first, rows):
    return ref.at[slab // 2, pl.ds(2 * first + slab % 2, rows, stride=2), :]


def _ready_token(arrays):
    tiles = [a[i:i + SUBLANES, j:j + LANES]
             for a in arrays
             for i in range(0, a.shape[0], SUBLANES)
             for j in range(0, a.shape[1], LANES)]
    return functools.reduce(jnp.maximum, tiles)


def _ordered_after(x, token, every_tile=False):
    if token is None:
        return x
    as_int = token.astype(jnp.int32)
    zero = lax.shift_right_logical(lax.shift_right_logical(as_int, jnp.int32(16)), jnp.int32(16))
    zero = zero.astype(F32)
    if every_tile:
        return x + jnp.concatenate([zero] * (x.shape[0] // SUBLANES), axis=0)
    head = x[:SUBLANES, :LANES] + zero
    if x.shape[1] > LANES:
        head = jnp.concatenate([head, x[:SUBLANES, LANES:]], axis=1)
    if x.shape[0] > SUBLANES:
        head = jnp.concatenate([head, x[SUBLANES:]], axis=0)
    return head


def _stencil_chunk(r, ubuf, pbuf, act_ref, mix_ref, dw_ref, dwb_ref, lng_ref, lnb_ref,
                   first_pos, anchor):
    n_slabs = D_CONV // LANES
    base = r * STENCIL_ROWS
    conv, conv_done = [], anchor
    for s in range(n_slabs):
        lanes = slice(s * LANES, (s + 1) * LANES)
        acc = _ordered_after(jnp.broadcast_to(dwb_ref[:, lanes], (STENCIL_ROWS, LANES)),
                             conv_done, every_tile=True)
        for k in range(CONV_WIDTH):
            first = base + CONV_HALO - (CONV_WIDTH - 1) + k
            acc = acc + _slab_rows(ubuf, s, first, STENCIL_ROWS)[...] * dw_ref[k:k + 1, lanes]
        conv.append(acc)
        conv_done = _ready_token([acc])
    mu = jnp.sum(sum(conv), axis=-1, keepdims=True) * (1.0 / D_CONV)
    cen = [c - mu for c in conv]
    var = jnp.sum(sum(c * c for c in cen), axis=-1, keepdims=True) * (1.0 / D_CONV)
    inv = lax.rsqrt(var + LN_EPS)
    results = []
    for s in range(n_slabs):
        lanes = slice(s * LANES, (s + 1) * LANES)
        ln = (cen[s] * inv) * lng_ref[:, lanes] + lnb_ref[:, lanes]
        act = ln * jax.nn.sigmoid(ln)
        act_ref[base:base + STENCIL_ROWS, lanes] = act.astype(BF16)
        results.append(act)
    pos1 = (first_pos + base + 1
            + lax.broadcasted_iota(jnp.int32, (STENCIL_ROWS, 1), 0)).astype(F32)
    for gi, w in enumerate(POOL_WINDOWS):
        cur = _ordered_after(_slab_rows(pbuf, gi, POOL_HALO + base, STENCIL_ROWS)[...], anchor)
        tot = cur
        for j in range(1, w):
            tot = tot + _slab_rows(pbuf, gi, POOL_HALO + base - j, STENCIL_ROWS)[...]
        mixed = tot * (1.0 / jnp.minimum(pos1, float(w))) - cur
        mix_ref[base:base + STENCIL_ROWS, gi * POOL_GROUP:(gi + 1) * POOL_GROUP] = (
            mixed.astype(BF16))
        results.append(mixed)
    return _ready_token(results), conv_done


def _front_kernel(x_ref, n1_ref, wg_ref, wu_ref, wd_ref, nm_ref, win_ref, dw_ref, dwb_ref,
                  lng_ref, lnb_ref, late0_ref, late1_ref, late2_ref, late3_ref,
                  x1_ref, act_ref, mix_ref, cast0_ref, cast1_ref, cast2_ref, cast3_ref,
                  ubuf, pbuf, *, tiles_per_seq):
    i = pl.program_id(0)

    for src, dst in ((late0_ref, cast0_ref), (late1_ref, cast1_ref), (late2_ref, cast2_ref),
                     (late3_ref, cast3_ref)):
        dst[...] = src[...].astype(BF16)
    n_gate, n_stencil = wg_ref.shape[1] // MXU_COLS, TILE // STENCIL_ROWS
    n_early = n_stencil - STENCILS_UNDER_DOWN

    @pl.when(i == 0)
    def _():
        ubuf[...] = jnp.zeros(ubuf.shape, F32)
        pbuf[...] = jnp.zeros(pbuf.shape, F32)

    @pl.when(lax.rem(i, tiles_per_seq) == 1)
    def _():
        ubuf[:, 0:2 * CONV_HALO, :] = jnp.zeros((2, 2 * CONV_HALO, LANES), F32)
        pbuf[:, 0:2 * POOL_HALO, :] = jnp.zeros((2, 2 * POOL_HALO, LANES), F32)

    prev_first_pos = lax.rem(i + tiles_per_seq - 1, tiles_per_seq) * TILE
    stencil_chunk = functools.partial(
        _stencil_chunk, ubuf=ubuf, pbuf=pbuf, act_ref=act_ref, mix_ref=mix_ref, dw_ref=dw_ref,
        dwb_ref=dwb_ref, lng_ref=lng_ref, lnb_ref=lnb_ref, first_pos=prev_first_pos)

    x = x_ref[...]
    h = _rmsnorm(x, n1_ref[...]).astype(BF16)
    hid, prev = [], None
    for c in range(n_gate):
        cols = slice(c * MXU_COLS, (c + 1) * MXU_COLS)
        gate = jnp.dot(h, wg_ref[:, cols], preferred_element_type=F32)
        up = jnp.dot(h, wu_ref[:, cols], preferred_element_type=F32)
        act = gate * jax.nn.sigmoid(gate) * up
        for r in range(c * n_early // n_gate, (c + 1) * n_early // n_gate):
            done, prev = stencil_chunk(r, anchor=prev)
            act = _ordered_after(act, done)
        hid.append(act.astype(BF16))
        prev = _ready_token([gate[:SUBLANES, :LANES]] + ([] if prev is None else [prev]))
    hid = jnp.concatenate(hid, axis=1)
    x1 = x + FFN_RES_WEIGHT * jnp.dot(hid, wd_ref[...], preferred_element_type=F32)
    for r in range(n_early, n_stencil):
        done, prev = stencil_chunk(r, anchor=prev)
        x1 = _ordered_after(x1, done)
    x1_ref[...] = x1

    h = _rmsnorm(x1, nm_ref[...]).astype(BF16)
    proj = jnp.dot(h, win_ref[...], preferred_element_type=F32)
    u = proj[:, :D_CONV] * jax.nn.sigmoid(proj[:, D_CONV:2 * D_CONV])
    p = proj[:, 2 * D_CONV:]

    ubuf[:, 0:2 * CONV_HALO, :] = ubuf[:, 2 * TILE:2 * (TILE + CONV_HALO), :]
    pbuf[:, 0:2 * POOL_HALO, :] = pbuf[:, 2 * TILE:2 * (TILE + POOL_HALO), :]
    for s in range(D_CONV // LANES):
        _slab_rows(ubuf, s, CONV_HALO, TILE)[...] = u[:, s * LANES:(s + 1) * LANES]
    for s in range(D_POOL // LANES):
        _slab_rows(pbuf, s, POOL_HALO, TILE)[...] = p[:, s * LANES:(s + 1) * LANES]


def _front_call(x2d, n1, wg, wu, wd, nm, win, dw, dwb, lng, lnb, late_weights, *,
                tiles_per_seq):
    n, d = x2d.shape
    ff = wg.shape[1]
    nt = n // TILE
    resident = pl.Buffered(1)
    const = lambda i: (0, 0)
    cur = lambda i: (jnp.minimum(i, nt - 1), 0)
    prev = lambda i: (jnp.maximum(i - 1, 0), 0)

    def late_block(w):
        rows = next(r for r in range(BF16_SUBLANES, w.shape[0] + 1, BF16_SUBLANES)
                    if w.shape[0] % r == 0 and w.shape[0] // r <= nt)
        return pl.BlockSpec((rows, w.shape[1]),
                            lambda i: (jnp.minimum(i, w.shape[0] // rows - 1), 0))

    return pl.pallas_call(
        functools.partial(_front_kernel, tiles_per_seq=tiles_per_seq),
        grid=(nt + 1,),
        in_specs=[
            pl.BlockSpec((TILE, d), cur),
            pl.BlockSpec((1, d), const),
            pl.BlockSpec((d, ff), const, pipeline_mode=resident),
            pl.BlockSpec((d, ff), const, pipeline_mode=resident),
            pl.BlockSpec((ff, d), const, pipeline_mode=resident),
            pl.BlockSpec((1, d), const),
            pl.BlockSpec((d, D_IN), const, pipeline_mode=resident),
            pl.BlockSpec((CONV_WIDTH, D_CONV), const),
            pl.BlockSpec((1, D_CONV), const),
            pl.BlockSpec((1, D_CONV), const),
            pl.BlockSpec((1, D_CONV), const),
        ] + [late_block(w) for w in late_weights],
        out_specs=[
            pl.BlockSpec((TILE, d), cur),
            pl.BlockSpec((TILE, D_CONV), prev),
            pl.BlockSpec((TILE, D_POOL), prev),
        ] + [late_block(w) for w in late_weights],
        out_shape=[
            jax.ShapeDtypeStruct((n, d), F32),
            jax.ShapeDtypeStruct((n, D_CONV), BF16),
            jax.ShapeDtypeStruct((n, D_POOL), BF16),
        ] + [jax.ShapeDtypeStruct(w.shape, BF16) for w in late_weights],
        scratch_shapes=[
            pltpu.VMEM((D_CONV // (2 * LANES), 2 * (CONV_HALO + TILE), LANES), F32),
            pltpu.VMEM((D_POOL // (2 * LANES), 2 * (POOL_HALO + TILE), LANES), F32),
        ],
        compiler_params=pltpu.CompilerParams(
            dimension_semantics=("arbitrary",),
            vmem_limit_bytes=V7X_VMEM_LIMIT_BYTES),
        name="front",
    )(x2d, n1, wg, wu, wd, nm, win, dw, dwb, lng, lnb, *late_weights)


def _back_kernel(x1_ref, act_ref, mix_ref, pw_ref, poolw_ref, pscale_ref, wout_ref, n2_ref,
                 wg_ref, wu_ref, wd_ref, nf_ref, y_ref):
    conv_out = jnp.dot(act_ref[...], pw_ref[...], preferred_element_type=F32)
    pool_parts = [
        jnp.dot(mix_ref[:, gi * POOL_GROUP:(gi + 1) * POOL_GROUP], poolw_ref[gi],
                preferred_element_type=F32)
        for gi in range(len(POOL_WINDOWS))]
    pool_out = jnp.concatenate(pool_parts, axis=-1) * pscale_ref[...]
    both = jnp.concatenate([conv_out, pool_out], axis=-1).astype(BF16)
    x2 = x1_ref[...] + jnp.dot(both, wout_ref[...], preferred_element_type=F32)
    y = _swiglu_residual(x2, n2_ref, wg_ref, wu_ref, wd_ref)
    y_ref[...] = _rmsnorm(y, nf_ref[...])


def _back_call(x1, act, mix, pw, poolw, pscale, wout, n2, wg, wu, wd, nf):
    n, d = x1.shape
    ff = wg.shape[1]
    resident = pl.Buffered(1)
    const = lambda i: (0, 0)
    tile = lambda i: (i, 0)
    return pl.pallas_call(
        _back_kernel,
        grid=(n // BACK_TILE,),
        in_specs=[
            pl.BlockSpec((BACK_TILE, d), tile),
            pl.BlockSpec((BACK_TILE, D_CONV), tile),
            pl.BlockSpec((BACK_TILE, D_POOL), tile),
            pl.BlockSpec((D_CONV, D_CONV), const, pipeline_mode=resident),
            pl.BlockSpec((len(POOL_WINDOWS), POOL_GROUP, POOL_GROUP), lambda i: (0, 0, 0)),
            pl.BlockSpec((1, D_POOL), const),
            pl.BlockSpec((d, d), const, pipeline_mode=resident),
            pl.BlockSpec((1, d), const),
            pl.BlockSpec((d, ff), const, pipeline_mode=resident),
            pl.BlockSpec((d, ff), const, pipeline_mode=resident),
            pl.BlockSpec((ff, d), const, pipeline_mode=resident),
            pl.BlockSpec((1, d), const),
        ],
        out_specs=pl.BlockSpec((BACK_TILE, d), tile),
        out_shape=jax.ShapeDtypeStruct((n, d), F32),
        compiler_params=pltpu.CompilerParams(
            dimension_semantics=("arbitrary",),
            vmem_limit_bytes=V7X_VMEM_LIMIT_BYTES),
        name="back",
    )(x1, act, mix, pw, poolw, pscale, wout, n2, wg, wu, wd, nf)


def kernel(x, ffn1_norm, ffn1_w_gate, ffn1_w_up, ffn1_w_down, mix_norm, w_in, conv_dw, conv_dw_b, conv_ln_g, conv_ln_b, conv_pw, pool_w, pool_scale, w_out, ffn2_norm, ffn2_w_gate, ffn2_w_up, ffn2_w_down, final_norm):
    b, t, d = x.shape
    assert t % TILE == 0 and t // TILE >= 2
    row = lambda v: v.reshape(1, -1)
    bf = lambda w: w.astype(BF16)

    late = [ffn2_w_gate, ffn2_w_up, ffn2_w_down, w_out]
    x1, act, mix, wg2, wu2, wd2, wout = _front_call(
        x.reshape(b * t, d), row(ffn1_norm), bf(ffn1_w_gate), bf(ffn1_w_up), bf(ffn1_w_down),
        row(mix_norm), bf(w_in), conv_dw, row(conv_dw_b), row(conv_ln_g), row(conv_ln_b),
        late, tiles_per_seq=t // TILE)
    y = _back_call(
        x1, act, mix, bf(conv_pw), bf(pool_w), row(pool_scale), wout, row(ffn2_norm),
        wg2, wu2, wd2, row(final_norm))
    return y.reshape(b, t, d)
```

```python
import functools

import jax
import jax.numpy as jnp
from jax import lax
from jax.experimental import pallas as pl
from jax.experimental.pallas import tpu as pltpu

D_CONV = 512
D_POOL = 512
CONV_WIDTH = 31
POOL_WINDOWS = (2, 4, 8, 16)
POOL_GROUP = D_POOL // len(POOL_WINDOWS)
D_IN = 2 * D_CONV + D_POOL
RMS_EPS = 1e-6
LN_EPS = 1e-5
FFN_RES_WEIGHT = 0.5

V7X_VMEM_LIMIT_BYTES = 60 * 1024 * 1024
SUBLANES = 8
BF16_SUBLANES = 16
LANES = 128

TILE = 512
BACK_TILE = 1024
CONV_HALO = 32
POOL_HALO = 16
STENCIL_ROWS = 16
MXU_COLS = 256
STENCILS_UNDER_DOWN = 14

BF16 = jnp.bfloat16
F32 = jnp.float32


def _rmsnorm(x, g):
    r = lax.rsqrt(jnp.mean(x * x, axis=-1, keepdims=True) + RMS_EPS)
    return (x * r) * g


def _swiglu_residual(x, ng_ref, wg_ref, wu_ref, wd_ref):
    h = _rmsnorm(x, ng_ref[...]).astype(BF16)
    gate = jnp.dot(h, wg_ref[...], preferred_element_type=F32)
    up = jnp.dot(h, wu_ref[...], preferred_element_type=F32)
    hid = (gate * jax.nn.sigmoid(gate) * up).astype(BF16)
    return x + FFN_RES_WEIGHT * jnp.dot(hid, wd_ref[...], preferred_element_type=F32)


def _slab_rows(ref, slab, first, rows):
    return ref.at[slab // 2, pl.ds(2 * first + slab % 2, rows, stride=2), :]


def _ready_token(arrays):
    tiles = [a[i:i + SUBLANES, j:j + LANES]
             for a in arrays
             for i in range(0, a.shape[0], SUBLANES)
             for j in range(0, a.shape[1], LANES)]
    return functools.reduce(jnp.maximum, tiles)


def _ordered_after(x, token, every_tile=False):
    if token is None:
        return x
    as_int = token.astype(jnp.int32)
    zero = lax.shift_right_logical(lax.shift_right_logical(as_int, jnp.int32(16)), jnp.int32(16))
    zero = zero.astype(F32)
    if every_tile:
        return x + jnp.concatenate([zero] * (x.shape[0] // SUBLANES), axis=0)
    head = x[:SUBLANES, :LANES] + zero
    if x.shape[1] > LANES:
        head = jnp.concatenate([head, x[:SUBLANES, LANES:]], axis=1)
    if x.shape[0] > SUBLANES:
        head = jnp.concatenate([head, x[SUBLANES:]], axis=0)
    return head


def _stencil_chunk(r, ubuf, pbuf, act_ref, mix_ref, dw_ref, dwb_ref, lng_ref, lnb_ref,
                   first_pos, anchor):
    n_slabs = D_CONV // LANES
    base = r * STENCIL_ROWS
    conv, conv_done = [], anchor
    for s in range(n_slabs):
        lanes = slice(s * LANES, (s + 1) * LANES)
        acc = _ordered_after(jnp.broadcast_to(dwb_ref[:, lanes], (STENCIL_ROWS, LANES)),
                             conv_done, every_tile=True)
        for k in range(CONV_WIDTH):
            first = base + CONV_HALO - (CONV_WIDTH - 1) + k
            acc = acc + _slab_rows(ubuf, s, first, STENCIL_ROWS)[...] * dw_ref[k:k + 1, lanes]
        conv.append(acc)
        conv_done = _ready_token([acc])
    mu = jnp.sum(sum(conv), axis=-1, keepdims=True) * (1.0 / D_CONV)
    cen = [c - mu for c in conv]
    var = jnp.sum(sum(c * c for c in cen), axis=-1, keepdims=True) * (1.0 / D_CONV)
    inv = lax.rsqrt(var + LN_EPS)
    results = []
    for s in range(n_slabs):
        lanes = slice(s * LANES, (s + 1) * LANES)
        ln = (cen[s] * inv) * lng_ref[:, lanes] + lnb_ref[:, lanes]
        act = ln * jax.nn.sigmoid(ln)
        act_ref[base:base + STENCIL_ROWS, lanes] = act.astype(BF16)
        results.append(act)
    pos1 = (first_pos + base + 1
            + lax.broadcasted_iota(jnp.int32, (STENCIL_ROWS, 1), 0)).astype(F32)
    for gi, w in enumerate(POOL_WINDOWS):
        cur = _ordered_after(_slab_rows(pbuf, gi, POOL_HALO + base, STENCIL_ROWS)[...], anchor)
        tot = cur
        for j in range(1, w):
            tot = tot + _slab_rows(pbuf, gi, POOL_HALO + base - j, STENCIL_ROWS)[...]
        mixed = tot * (1.0 / jnp.minimum(pos1, float(w))) - cur
        mix_ref[base:base + STENCIL_ROWS, gi * POOL_GROUP:(gi + 1) * POOL_GROUP] = (
            mixed.astype(BF16))
        results.append(mixed)
    return _ready_token(results), conv_done


def _front_kernel(x_ref, n1_ref, wg_ref, wu_ref, wd_ref, nm_ref, win_ref, dw_ref, dwb_ref,
                  lng_ref, lnb_ref, pw_ref, poolw_ref, pscale_ref, wout_ref,
                  late0_ref, late1_ref, late2_ref,
                  x1_ref, actmix_ref, wmix_ref, cast0_ref, cast1_ref, cast2_ref,
                  ubuf, pbuf, *, tiles_per_seq):
    i = pl.program_id(0)
    act_ref, mix_ref = actmix_ref.at[:, 0:D_CONV], actmix_ref.at[:, D_CONV:D_CONV + D_POOL]

    for src, dst in ((late0_ref, cast0_ref), (late1_ref, cast1_ref), (late2_ref, cast2_ref)):
        dst[...] = src[...].astype(BF16)
    n_gate, n_stencil = wg_ref.shape[1] // MXU_COLS, TILE // STENCIL_ROWS
    n_early = n_stencil - STENCILS_UNDER_DOWN

    @pl.when(i == 0)
    def _():
        ubuf[...] = jnp.zeros(ubuf.shape, F32)
        pbuf[...] = jnp.zeros(pbuf.shape, F32)
        fold = functools.partial(jnp.dot, preferred_element_type=F32,
                                 precision=lax.Precision.HIGHEST)
        wmix_ref[0:D_CONV, :] = fold(pw_ref[...], wout_ref[0:D_CONV, :]).astype(BF16)
        for gi in range(len(POOL_WINDOWS)):
            lanes = slice(gi * POOL_GROUP, (gi + 1) * POOL_GROUP)
            rows = slice(D_CONV + gi * POOL_GROUP, D_CONV + (gi + 1) * POOL_GROUP)
            wmix_ref[rows, :] = fold(poolw_ref[gi] * pscale_ref[:, lanes],
                                     wout_ref[rows, :]).astype(BF16)

    @pl.when(lax.rem(i, tiles_per_seq) == 1)
    def _():
        ubuf[:, 0:2 * CONV_HALO, :] = jnp.zeros((2, 2 * CONV_HALO, LANES), F32)
        pbuf[:, 0:2 * POOL_HALO, :] = jnp.zeros((2, 2 * POOL_HALO, LANES), F32)

    prev_first_pos = lax.rem(i + tiles_per_seq - 1, tiles_per_seq) * TILE
    stencil_chunk = functools.partial(
        _stencil_chunk, ubuf=ubuf, pbuf=pbuf, act_ref=act_ref, mix_ref=mix_ref, dw_ref=dw_ref,
        dwb_ref=dwb_ref, lng_ref=lng_ref, lnb_ref=lnb_ref, first_pos=prev_first_pos)

    x = x_ref[...]
    h = _rmsnorm(x, n1_ref[...]).astype(BF16)
    hid, prev = [], None
    for c in range(n_gate):
        cols = slice(c * MXU_COLS, (c + 1) * MXU_COLS)
        gate = jnp.dot(h, wg_ref[:, cols], preferred_element_type=F32)
        up = jnp.dot(h, wu_ref[:, cols], preferred_element_type=F32)
        act = gate * jax.nn.sigmoid(gate) * up
        for r in range(c * n_early // n_gate, (c + 1) * n_early // n_gate):
            done, prev = stencil_chunk(r, anchor=prev)
            act = _ordered_after(act, done)
        hid.append(act.astype(BF16))
        prev = _ready_token([gate[:SUBLANES, :LANES]] + ([] if prev is None else [prev]))
    hid = jnp.concatenate(hid, axis=1)
    x1 = x + FFN_RES_WEIGHT * jnp.dot(hid, wd_ref[...], preferred_element_type=F32)
    for r in range(n_early, n_stencil):
        done, prev = stencil_chunk(r, anchor=prev)
        x1 = _ordered_after(x1, done)
    x1_ref[...] = x1

    h = _rmsnorm(x1, nm_ref[...]).astype(BF16)
    proj = jnp.dot(h, win_ref[...], preferred_element_type=F32)
    u = proj[:, :D_CONV] * jax.nn.sigmoid(proj[:, D_CONV:2 * D_CONV])
    p = proj[:, 2 * D_CONV:]

    ubuf[:, 0:2 * CONV_HALO, :] = ubuf[:, 2 * TILE:2 * (TILE + CONV_HALO), :]
    pbuf[:, 0:2 * POOL_HALO, :] = pbuf[:, 2 * TILE:2 * (TILE + POOL_HALO), :]
    for s in range(D_CONV // LANES):
        _slab_rows(ubuf, s, CONV_HALO, TILE)[...] = u[:, s * LANES:(s + 1) * LANES]
    for s in range(D_POOL // LANES):
        _slab_rows(pbuf, s, POOL_HALO, TILE)[...] = p[:, s * LANES:(s + 1) * LANES]


def _front_call(x2d, n1, wg, wu, wd, nm, win, dw, dwb, lng, lnb, pw, poolw, pscale, wout,
                late_weights, *, tiles_per_seq):
    n, d = x2d.shape
    ff = wg.shape[1]
    nt = n // TILE
    resident = pl.Buffered(1)
    const = lambda i: (0, 0)
    cur = lambda i: (jnp.minimum(i, nt - 1), 0)
    prev = lambda i: (jnp.maximum(i - 1, 0), 0)

    def late_block(w):
        rows = next(r for r in range(BF16_SUBLANES, w.shape[0] + 1, BF16_SUBLANES)
                    if w.shape[0] % r == 0 and w.shape[0] // r <= nt)
        return pl.BlockSpec((rows, w.shape[1]),
                            lambda i: (jnp.minimum(i, w.shape[0] // rows - 1), 0))

    return pl.pallas_call(
        functools.partial(_front_kernel, tiles_per_seq=tiles_per_seq),
        grid=(nt + 1,),
        in_specs=[
            pl.BlockSpec((TILE, d), cur),
            pl.BlockSpec((1, d), const),
            pl.BlockSpec((d, ff), const, pipeline_mode=resident),
            pl.BlockSpec((d, ff), const, pipeline_mode=resident),
            pl.BlockSpec((ff, d), const, pipeline_mode=resident),
            pl.BlockSpec((1, d), const),
            pl.BlockSpec((d, D_IN), const, pipeline_mode=resident),
            pl.BlockSpec((CONV_WIDTH, D_CONV), const),
            pl.BlockSpec((1, D_CONV), const),
            pl.BlockSpec((1, D_CONV), const),
            pl.BlockSpec((1, D_CONV), const),
            pl.BlockSpec((D_CONV, D_CONV), const, pipeline_mode=resident),
            pl.BlockSpec((len(POOL_WINDOWS), POOL_GROUP, POOL_GROUP), lambda i: (0, 0, 0),
                         pipeline_mode=resident),
            pl.BlockSpec((1, D_POOL), const),
            pl.BlockSpec((D_CONV + D_POOL, d), const, pipeline_mode=resident),
        ] + [late_block(w) for w in late_weights],
        out_specs=[
            pl.BlockSpec((TILE, d), cur),
            pl.BlockSpec((TILE, D_CONV + D_POOL), prev),
            pl.BlockSpec((D_CONV + D_POOL, d), const),
        ] + [late_block(w) for w in late_weights],
        out_shape=[
            jax.ShapeDtypeStruct((n, d), F32),
            jax.ShapeDtypeStruct((n, D_CONV + D_POOL), BF16),
            jax.ShapeDtypeStruct((D_CONV + D_POOL, d), BF16),
        ] + [jax.ShapeDtypeStruct(w.shape, BF16) for w in late_weights],
        scratch_shapes=[
            pltpu.VMEM((D_CONV // (2 * LANES), 2 * (CONV_HALO + TILE), LANES), F32),
            pltpu.VMEM((D_POOL // (2 * LANES), 2 * (POOL_HALO + TILE), LANES), F32),
        ],
        compiler_params=pltpu.CompilerParams(
            dimension_semantics=("arbitrary",),
            vmem_limit_bytes=V7X_VMEM_LIMIT_BYTES),
        name="front",
    )(x2d, n1, wg, wu, wd, nm, win, dw, dwb, lng, lnb, pw, poolw, pscale, wout, *late_weights)


def _back_kernel(x1_ref, actmix_ref, wmix_ref, n2_ref, wg_ref, wu_ref, wd_ref, nf_ref, y_ref):
    x2 = x1_ref[...] + jnp.dot(actmix_ref[...], wmix_ref[...], preferred_element_type=F32)
    y = _swiglu_residual(x2, n2_ref, wg_ref, wu_ref, wd_ref)
    y_ref[...] = _rmsnorm(y, nf_ref[...])


def _back_call(x1, actmix, wmix, n2, wg, wu, wd, nf):
    n, d = x1.shape
    ff = wg.shape[1]
    resident = pl.Buffered(1)
    const = lambda i: (0, 0)
    tile = lambda i: (i, 0)
    return pl.pallas_call(
        _back_kernel,
        grid=(n // BACK_TILE,),
        in_specs=[
            pl.BlockSpec((BACK_TILE, d), tile),
            pl.BlockSpec((BACK_TILE, D_CONV + D_POOL), tile),
            pl.BlockSpec((D_CONV + D_POOL, d), const, pipeline_mode=resident),
            pl.BlockSpec((1, d), const),
            pl.BlockSpec((d, ff), const, pipeline_mode=resident),
            pl.BlockSpec((d, ff), const, pipeline_mode=resident),
            pl.BlockSpec((ff, d), const, pipeline_mode=resident),
            pl.BlockSpec((1, d), const),
        ],
        out_specs=pl.BlockSpec((BACK_TILE, d), tile),
        out_shape=jax.ShapeDtypeStruct((n, d), F32),
        compiler_params=pltpu.CompilerParams(
            dimension_semantics=("arbitrary",),
            vmem_limit_bytes=V7X_VMEM_LIMIT_BYTES),
        name="back",
    )(x1, actmix, wmix, n2, wg, wu, wd, nf)


def kernel(x, ffn1_norm, ffn1_w_gate, ffn1_w_up, ffn1_w_down, mix_norm, w_in, conv_dw, conv_dw_b, conv_ln_g, conv_ln_b, conv_pw, pool_w, pool_scale, w_out, ffn2_norm, ffn2_w_gate, ffn2_w_up, ffn2_w_down, final_norm):
    b, t, d = x.shape
    assert t % TILE == 0 and t // TILE >= 2
    row = lambda v: v.reshape(1, -1)
    bf = lambda w: w.astype(BF16)

    late = [ffn2_w_gate, ffn2_w_up, ffn2_w_down]
    x1, actmix, wmix, wg2, wu2, wd2 = _front_call(
        x.reshape(b * t, d), row(ffn1_norm), bf(ffn1_w_gate), bf(ffn1_w_up), bf(ffn1_w_down),
        row(mix_norm), bf(w_in), conv_dw, row(conv_dw_b), row(conv_ln_g), row(conv_ln_b),
        conv_pw, pool_w, row(pool_scale), w_out, late, tiles_per_seq=t // TILE)
    y = _back_call(x1, actmix, wmix, row(ffn2_norm), wg2, wu2, wd2, row(final_norm))
    return y.reshape(b, t, d)
```

```python
import functools

import jax
import jax.numpy as jnp
from jax import lax
from jax.experimental import pallas as pl
from jax.experimental.pallas import tpu as pltpu

D_CONV = 512
D_POOL = 512
CONV_WIDTH = 31
POOL_WINDOWS = (2, 4, 8, 16)
POOL_GROUP = D_POOL // len(POOL_WINDOWS)
D_IN = 2 * D_CONV + D_POOL
RMS_EPS = 1e-6
LN_EPS = 1e-5
FFN_RES_WEIGHT = 0.5

V7X_VMEM_LIMIT_BYTES = 60 * 1024 * 1024
SUBLANES = 8
BF16_SUBLANES = 16
LANES = 128

TILE = 512
BACK_TILE = 1024
CONV_HALO = 32
POOL_HALO = 16
STENCIL_ROWS = 16
MXU_COLS = 256
STENCILS_UNDER_DOWN = 14
WEIGHT_STEPS = 8

BF16 = jnp.bfloat16
F32 = jnp.float32


def _rmsnorm(x, g):
    r = lax.rsqrt(jnp.mean(x * x, axis=-1, keepdims=True) + RMS_EPS)
    return (x * r) * g


def _swiglu_residual(x, ng_ref, wg_ref, wu_ref, wd_ref):
    h = _rmsnorm(x, ng_ref[...]).astype(BF16)
    gate = jnp.dot(h, wg_ref[...], preferred_element_type=F32)
    up = jnp.dot(h, wu_ref[...], preferred_element_type=F32)
    hid = (gate * jax.nn.sigmoid(gate) * up).astype(BF16)
    return x + FFN_RES_WEIGHT * jnp.dot(hid, wd_ref[...], preferred_element_type=F32)


def _slab_rows(ref, slab, first, rows):
    return ref.at[slab // 2, pl.ds(2 * first + slab % 2, rows, stride=2), :]


def _ready_token(arrays):
    tiles = [a[i:i + SUBLANES, j:j + LANES]
             for a in arrays
             for i in range(0, a.shape[0], SUBLANES)
             for j in range(0, a.shape[1], LANES)]
    return functools.reduce(jnp.maximum, tiles)


def _ordered_after(x, token, every_tile=False):
    if token is None:
        return x
    as_int = token.astype(jnp.int32)
    zero = lax.shift_right_logical(lax.shift_right_logical(as_int, jnp.int32(16)), jnp.int32(16))
    zero = zero.astype(F32)
    if every_tile:
        return x + jnp.concatenate([zero] * (x.shape[0] // SUBLANES), axis=0)
    head = x[:SUBLANES, :LANES] + zero
    if x.shape[1] > LANES:
        head = jnp.concatenate([head, x[:SUBLANES, LANES:]], axis=1)
    if x.shape[0] > SUBLANES:
        head = jnp.concatenate([head, x[SUBLANES:]], axis=0)
    return head


def _stencil_chunk(r, ubuf, pbuf, act_ref, mix_ref, dw_ref, dwb_ref, lng_ref, lnb_ref,
                   first_pos, anchor):
    n_slabs = D_CONV // LANES
    base = r * STENCIL_ROWS
    conv, conv_done = [], anchor
    for s in range(n_slabs):
        lanes = slice(s * LANES, (s + 1) * LANES)
        acc = _ordered_after(jnp.broadcast_to(dwb_ref[:, lanes], (STENCIL_ROWS, LANES)),
                             conv_done, every_tile=True)
        for k in range(CONV_WIDTH):
            first = base + CONV_HALO - (CONV_WIDTH - 1) + k
            acc = acc + _slab_rows(ubuf, s, first, STENCIL_ROWS)[...] * dw_ref[k:k + 1, lanes]
        conv.append(acc)
        conv_done = _ready_token([acc])
    mu = jnp.sum(sum(conv), axis=-1, keepdims=True) * (1.0 / D_CONV)
    cen = [c - mu for c in conv]
    var = jnp.sum(sum(c * c for c in cen), axis=-1, keepdims=True) * (1.0 / D_CONV)
    inv = lax.rsqrt(var + LN_EPS)
    results = []
    for s in range(n_slabs):
        lanes = slice(s * LANES, (s + 1) * LANES)
        ln = (cen[s] * inv) * lng_ref[:, lanes] + lnb_ref[:, lanes]
        act = ln * jax.nn.sigmoid(ln)
        act_ref[base:base + STENCIL_ROWS, lanes] = act.astype(BF16)
        results.append(act)
    pos1 = (first_pos + base + 1
            + lax.broadcasted_iota(jnp.int32, (STENCIL_ROWS, 1), 0)).astype(F32)
    for gi, w in enumerate(POOL_WINDOWS):
        cur = _ordered_after(_slab_rows(pbuf, gi, POOL_HALO + base, STENCIL_ROWS)[...], anchor)
        tot = cur
        for j in range(1, w):
            tot = tot + _slab_rows(pbuf, gi, POOL_HALO + base - j, STENCIL_ROWS)[...]
        mixed = tot * (1.0 / jnp.minimum(pos1, float(w))) - cur
        mix_ref[base:base + STENCIL_ROWS, gi * POOL_GROUP:(gi + 1) * POOL_GROUP] = (
            mixed.astype(BF16))
        results.append(mixed)
    return _ready_token(results), conv_done


def _front_kernel(x_ref, n1_ref, wg32_ref, wu32_ref, wd32_ref, nm_ref, win32_ref, dw_ref,
                  dwb_ref, lng_ref, lnb_ref, pw_ref, poolw_ref, pscale_ref, wout_ref,
                  late0_ref, late1_ref, late2_ref,
                  x1_ref, actmix_ref, wmix_ref, cast0_ref, cast1_ref, cast2_ref,
                  wg_ref, wu_ref, wd_ref, win_ref, ubuf, pbuf, *, tiles_per_seq):
    step = pl.program_id(0)

    @pl.when(step < WEIGHT_STEPS)
    def _():
        for src, dst in ((wg32_ref, wg_ref), (wu32_ref, wu_ref), (wd32_ref, wd_ref),
                         (win32_ref, win_ref)):
            rows = src.shape[0]
            first = pl.multiple_of(step * rows, BF16_SUBLANES)
            dst[pl.ds(first, rows), :] = src[...].astype(BF16)

    @pl.when(step >= WEIGHT_STEPS)
    def _():
        _front_step(step - WEIGHT_STEPS, x_ref, n1_ref, wg_ref, wu_ref, wd_ref, nm_ref, win_ref,
                    dw_ref, dwb_ref, lng_ref, lnb_ref, pw_ref, poolw_ref, pscale_ref, wout_ref,
                    late0_ref, late1_ref, late2_ref, x1_ref, actmix_ref, wmix_ref, cast0_ref,
                    cast1_ref, cast2_ref, ubuf, pbuf, tiles_per_seq=tiles_per_seq)


def _front_step(i, x_ref, n1_ref, wg_ref, wu_ref, wd_ref, nm_ref, win_ref, dw_ref, dwb_ref,
                lng_ref, lnb_ref, pw_ref, poolw_ref, pscale_ref, wout_ref,
                late0_ref, late1_ref, late2_ref,
                x1_ref, actmix_ref, wmix_ref, cast0_ref, cast1_ref, cast2_ref,
                ubuf, pbuf, *, tiles_per_seq):
    act_ref, mix_ref = actmix_ref.at[:, 0:D_CONV], actmix_ref.at[:, D_CONV:D_CONV + D_POOL]

    for src, dst in ((late0_ref, cast0_ref), (late1_ref, cast1_ref), (late2_ref, cast2_ref)):
        dst[...] = src[...].astype(BF16)
    n_gate, n_stencil = wg_ref.shape[1] // MXU_COLS, TILE // STENCIL_ROWS
    n_early = n_stencil - STENCILS_UNDER_DOWN

    @pl.when(i == 0)
    def _():
        ubuf[...] = jnp.zeros(ubuf.shape, F32)
        pbuf[...] = jnp.zeros(pbuf.shape, F32)
        fold = functools.partial(jnp.dot, preferred_element_type=F32,
                                 precision=lax.Precision.HIGHEST)
        wmix_ref[0:D_CONV, :] = fold(pw_ref[...], wout_ref[0:D_CONV, :]).astype(BF16)
        for gi in range(len(POOL_WINDOWS)):
            lanes = slice(gi * POOL_GROUP, (gi + 1) * POOL_GROUP)
            rows = slice(D_CONV + gi * POOL_GROUP, D_CONV + (gi + 1) * POOL_GROUP)
            wmix_ref[rows, :] = fold(poolw_ref[gi] * pscale_ref[:, lanes],
                                     wout_ref[rows, :]).astype(BF16)

    @pl.when(lax.rem(i, tiles_per_seq) == 1)
    def _():
        ubuf[:, 0:2 * CONV_HALO, :] = jnp.zeros((2, 2 * CONV_HALO, LANES), F32)
        pbuf[:, 0:2 * POOL_HALO, :] = jnp.zeros((2, 2 * POOL_HALO, LANES), F32)

    prev_first_pos = lax.rem(i + tiles_per_seq - 1, tiles_per_seq) * TILE
    stencil_chunk = functools.partial(
        _stencil_chunk, ubuf=ubuf, pbuf=pbuf, act_ref=act_ref, mix_ref=mix_ref, dw_ref=dw_ref,
        dwb_ref=dwb_ref, lng_ref=lng_ref, lnb_ref=lnb_ref, first_pos=prev_first_pos)

    x = x_ref[...]
    h = _rmsnorm(x, n1_ref[...]).astype(BF16)
    hid, prev = [], None
    for c in range(n_gate):
        cols = slice(c * MXU_COLS, (c + 1) * MXU_COLS)
        gate = jnp.dot(h, wg_ref[:, cols], preferred_element_type=F32)
        up = jnp.dot(h, wu_ref[:, cols], preferred_element_type=F32)
        act = gate * jax.nn.sigmoid(gate) * up
        for r in range(c * n_early // n_gate, (c + 1) * n_early // n_gate):
            done, prev = stencil_chunk(r, anchor=prev)
            act = _ordered_after(act, done)
        hid.append(act.astype(BF16))
        prev = _ready_token([gate[:SUBLANES, :LANES]] + ([] if prev is None else [prev]))
    hid = jnp.concatenate(hid, axis=1)
    x1 = x + FFN_RES_WEIGHT * jnp.dot(hid, wd_ref[...], preferred_element_type=F32)
    for r in range(n_early, n_stencil):
        done, prev = stencil_chunk(r, anchor=prev)
        x1 = _ordered_after(x1, done)
    x1_ref[...] = x1

    h = _rmsnorm(x1, nm_ref[...]).astype(BF16)
    proj = jnp.dot(h, win_ref[...], preferred_element_type=F32)
    u = proj[:, :D_CONV] * jax.nn.sigmoid(proj[:, D_CONV:2 * D_CONV])
    p = proj[:, 2 * D_CONV:]

    ubuf[:, 0:2 * CONV_HALO, :] = ubuf[:, 2 * TILE:2 * (TILE + CONV_HALO), :]
    pbuf[:, 0:2 * POOL_HALO, :] = pbuf[:, 2 * TILE:2 * (TILE + POOL_HALO), :]
    for s in range(D_CONV // LANES):
        _slab_rows(ubuf, s, CONV_HALO, TILE)[...] = u[:, s * LANES:(s + 1) * LANES]
    for s in range(D_POOL // LANES):
        _slab_rows(pbuf, s, POOL_HALO, TILE)[...] = p[:, s * LANES:(s + 1) * LANES]


def _front_call(x2d, n1, wg, wu, wd, nm, win, dw, dwb, lng, lnb, pw, poolw, pscale, wout,
                late_weights, *, tiles_per_seq):
    n, d = x2d.shape
    ff = wg.shape[1]
    nt = n // TILE
    resident = pl.Buffered(1)
    const = lambda s: (0, 0)
    cur = lambda s: (jnp.clip(s - WEIGHT_STEPS, 0, nt - 1), 0)
    prev = lambda s: (jnp.clip(s - WEIGHT_STEPS - 1, 0, nt - 1), 0)

    def own_block(w):
        assert w.shape[0] % (WEIGHT_STEPS * BF16_SUBLANES) == 0, w.shape
        return pl.BlockSpec((w.shape[0] // WEIGHT_STEPS, w.shape[1]),
                            lambda s: (jnp.minimum(s, WEIGHT_STEPS - 1), 0))

    def late_block(w):
        rows = next(r for r in range(BF16_SUBLANES, w.shape[0] + 1, BF16_SUBLANES)
                    if w.shape[0] % r == 0 and w.shape[0] // r <= nt)
        return pl.BlockSpec(
            (rows, w.shape[1]),
            lambda s: (jnp.clip(s - WEIGHT_STEPS, 0, w.shape[0] // rows - 1), 0))

    return pl.pallas_call(
        functools.partial(_front_kernel, tiles_per_seq=tiles_per_seq),
        grid=(WEIGHT_STEPS + nt + 1,),
        in_specs=[
            pl.BlockSpec((TILE, d), cur),
            pl.BlockSpec((1, d), const),
            own_block(wg),
            own_block(wu),
            own_block(wd),
            pl.BlockSpec((1, d), const),
            own_block(win),
            pl.BlockSpec((CONV_WIDTH, D_CONV), const),
            pl.BlockSpec((1, D_CONV), const),
            pl.BlockSpec((1, D_CONV), const),
            pl.BlockSpec((1, D_CONV), const),
            pl.BlockSpec((D_CONV, D_CONV), const, pipeline_mode=resident),
            pl.BlockSpec((len(POOL_WINDOWS), POOL_GROUP, POOL_GROUP), lambda s: (0, 0, 0),
                         pipeline_mode=resident),
            pl.BlockSpec((1, D_POOL), const),
            pl.BlockSpec((D_CONV + D_POOL, d), const, pipeline_mode=resident),
        ] + [late_block(w) for w in late_weights],
        out_specs=[
            pl.BlockSpec((TILE, d), cur),
            pl.BlockSpec((TILE, D_CONV + D_POOL), prev),
            pl.BlockSpec((D_CONV + D_POOL, d), const),
        ] + [late_block(w) for w in late_weights],
        out_shape=[
            jax.ShapeDtypeStruct((n, d), F32),
            jax.ShapeDtypeStruct((n, D_CONV + D_POOL), BF16),
            jax.ShapeDtypeStruct((D_CONV + D_POOL, d), BF16),
        ] + [jax.ShapeDtypeStruct(w.shape, BF16) for w in late_weights],
        scratch_shapes=[
            pltpu.VMEM(wg.shape, BF16),
            pltpu.VMEM(wu.shape, BF16),
            pltpu.VMEM(wd.shape, BF16),
            pltpu.VMEM(win.shape, BF16),
            pltpu.VMEM((D_CONV // (2 * LANES), 2 * (CONV_HALO + TILE), LANES), F32),
            pltpu.VMEM((D_POOL // (2 * LANES), 2 * (POOL_HALO + TILE), LANES), F32),
        ],
        compiler_params=pltpu.CompilerParams(
            dimension_semantics=("arbitrary",),
            vmem_limit_bytes=V7X_VMEM_LIMIT_BYTES),
        name="front",
    )(x2d, n1, wg, wu, wd, nm, win, dw, dwb, lng, lnb, pw, poolw, pscale, wout, *late_weights)


def _back_kernel(x1_ref, actmix_ref, wmix_ref, n2_ref, wg_ref, wu_ref, wd_ref, nf_ref, y_ref):
    x2 = x1_ref[...] + jnp.dot(actmix_ref[...], wmix_ref[...], preferred_element_type=F32)
    y = _swiglu_residual(x2, n2_ref, wg_ref, wu_ref, wd_ref)
    y_ref[...] = _rmsnorm(y, nf_ref[...])


def _back_call(x1, actmix, wmix, n2, wg, wu, wd, nf):
    n, d = x1.shape
    ff = wg.shape[1]
    resident = pl.Buffered(1)
    const = lambda i: (0, 0)
    tile = lambda i: (i, 0)
    return pl.pallas_call(
        _back_kernel,
        grid=(n // BACK_TILE,),
        in_specs=[
            pl.BlockSpec((BACK_TILE, d), tile),
            pl.BlockSpec((BACK_TILE, D_CONV + D_POOL), tile),
            pl.BlockSpec((D_CONV + D_POOL, d), const, pipeline_mode=resident),
            pl.BlockSpec((1, d), const),
            pl.BlockSpec((d, ff), const, pipeline_mode=resident),
            pl.BlockSpec((d, ff), const, pipeline_mode=resident),
            pl.BlockSpec((ff, d), const, pipeline_mode=resident),
            pl.BlockSpec((1, d), const),
        ],
        out_specs=pl.BlockSpec((BACK_TILE, d), tile),
        out_shape=jax.ShapeDtypeStruct((n, d), F32),
        compiler_params=pltpu.CompilerParams(
            dimension_semantics=("arbitrary",),
            vmem_limit_bytes=V7X_VMEM_LIMIT_BYTES),
        name="back",
    )(x1, actmix, wmix, n2, wg, wu, wd, nf)


def kernel(x, ffn1_norm, ffn1_w_gate, ffn1_w_up, ffn1_w_down, mix_norm, w_in, conv_dw, conv_dw_b, conv_ln_g, conv_ln_b, conv_pw, pool_w, pool_scale, w_out, ffn2_norm, ffn2_w_gate, ffn2_w_up, ffn2_w_down, final_norm):
    b, t, d = x.shape
    assert t % TILE == 0 and t // TILE >= 2
    row = lambda v: v.reshape(1, -1)

    late = [ffn2_w_gate, ffn2_w_up, ffn2_w_down]
    x1, actmix, wmix, wg2, wu2, wd2 = _front_call(
        x.reshape(b * t, d), row(ffn1_norm), ffn1_w_gate, ffn1_w_up, ffn1_w_down,
        row(mix_norm), w_in, conv_dw, row(conv_dw_b), row(conv_ln_g), row(conv_ln_b),
        conv_pw, pool_w, row(pool_scale), w_out, late, tiles_per_seq=t // TILE)
    y = _back_call(x1, actmix, wmix, row(ffn2_norm), wg2, wu2, wd2, row(final_norm))
    return y.reshape(b, t, d)
```

```python
import functools

import jax
import jax.numpy as jnp
from jax import lax
from jax.experimental import pallas as pl
from jax.experimental.pallas import tpu as pltpu

D_CONV = 512
D_POOL = 512
CONV_WIDTH = 31
POOL_WINDOWS = (2, 4, 8, 16)
POOL_GROUP = D_POOL // len(POOL_WINDOWS)
D_IN = 2 * D_CONV + D_POOL
RMS_EPS = 1e-6
LN_EPS = 1e-5
FFN_RES_WEIGHT = 0.5

V7X_VMEM_LIMIT_BYTES = 60 * 1024 * 1024
SUBLANES = 8
BF16_SUBLANES = 16
LANES = 128

TILE = 512
BACK_TILE = 1024
CONV_HALO = 32
POOL_HALO = 16
STENCIL_ROWS = 16
MXU_COLS = 256
STENCILS_UNDER_DOWN = 14
WEIGHT_STEPS = 8

BF16 = jnp.bfloat16
F32 = jnp.float32


def _rmsnorm(x, g):
    r = lax.rsqrt(jnp.mean(x * x, axis=-1, keepdims=True) + RMS_EPS)
    return (x * r) * g


def _swiglu_residual(x, ng_ref, wg_ref, wu_ref, wd_ref):
    h = _rmsnorm(x, ng_ref[...]).astype(BF16)
    gate = jnp.dot(h, wg_ref[...], preferred_element_type=F32)
    up = jnp.dot(h, wu_ref[...], preferred_element_type=F32)
    hid = (gate * jax.nn.sigmoid(gate) * up).astype(BF16)
    return x + FFN_RES_WEIGHT * jnp.dot(hid, wd_ref[...], preferred_element_type=F32)


def _slab_rows(ref, slab, first, rows):
    return ref.at[slab // 2, pl.ds(2 * first + slab % 2, rows, stride=2), :]


def _ready_token(arrays):
    tiles = [a[i:i + SUBLANES, j:j + LANES]
             for a in arrays
             for i in range(0, a.shape[0], SUBLANES)
             for j in range(0, a.shape[1], LANES)]
    return functools.reduce(jnp.maximum, tiles)


def _ordered_after(x, token, every_tile=False):
    if token is None:
        return x
    as_int = token.astype(jnp.int32)
    zero = lax.shift_right_logical(lax.shift_right_logical(as_int, jnp.int32(16)), jnp.int32(16))
    zero = zero.astype(F32)
    if every_tile:
        return x + jnp.concatenate([zero] * (x.shape[0] // SUBLANES), axis=0)
    head = x[:SUBLANES, :LANES] + zero
    if x.shape[1] > LANES:
        head = jnp.concatenate([head, x[:SUBLANES, LANES:]], axis=1)
    if x.shape[0] > SUBLANES:
        head = jnp.concatenate([head, x[SUBLANES:]], axis=0)
    return head


def _stencil_chunk(r, ubuf, pbuf, act_ref, mix_ref, dw_ref, dwb_ref, lng_ref, lnb_ref,
                   first_pos, anchor):
    n_slabs = D_CONV // LANES
    base = r * STENCIL_ROWS
    conv, conv_done = [], anchor
    for s in range(n_slabs):
        lanes = slice(s * LANES, (s + 1) * LANES)
        acc = _ordered_after(jnp.broadcast_to(dwb_ref[:, lanes], (STENCIL_ROWS, LANES)),
                             conv_done, every_tile=True)
        for k in range(CONV_WIDTH):
            first = base + CONV_HALO - (CONV_WIDTH - 1) + k
            acc = acc + _slab_rows(ubuf, s, first, STENCIL_ROWS)[...] * dw_ref[k:k + 1, lanes]
        conv.append(acc)
        conv_done = _ready_token([acc])
    mu = jnp.sum(sum(conv), axis=-1, keepdims=True) * (1.0 / D_CONV)
    cen = [c - mu for c in conv]
    var = jnp.sum(sum(c * c for c in cen), axis=-1, keepdims=True) * (1.0 / D_CONV)
    inv = lax.rsqrt(var + LN_EPS)
    results = []
    for s in range(n_slabs):
        lanes = slice(s * LANES, (s + 1) * LANES)
        ln = (cen[s] * inv) * lng_ref[:, lanes] + lnb_ref[:, lanes]
        act = ln * jax.nn.sigmoid(ln)
        act_ref[base:base + STENCIL_ROWS, lanes] = act.astype(BF16)
        results.append(act)
    pos1 = (first_pos + base + 1
            + lax.broadcasted_iota(jnp.int32, (STENCIL_ROWS, 1), 0)).astype(F32)
    for gi, w in enumerate(POOL_WINDOWS):
        cur = _ordered_after(_slab_rows(pbuf, gi, POOL_HALO + base, STENCIL_ROWS)[...], anchor)
        tot = cur
        for j in range(1, w):
            tot = tot + _slab_rows(pbuf, gi, POOL_HALO + base - j, STENCIL_ROWS)[...]
        mixed = tot * (1.0 / jnp.minimum(pos1, float(w))) - cur
        mix_ref[base:base + STENCIL_ROWS, gi * POOL_GROUP:(gi + 1) * POOL_GROUP] = (
            mixed.astype(BF16))
        results.append(mixed)
    return _ready_token(results), conv_done


def _front_kernel(x_ref, n1_ref, wg32_ref, wu32_ref, wd32_ref, nm_ref, win32_ref, dw_ref,
                  dwb_ref, lng_ref, lnb_ref, pw_ref, poolw_ref, pscale_ref, wout_ref,
                  late0_ref, late1_ref, late2_ref,
                  x1_ref, actmix_ref, wmix_ref, cast0_ref, cast1_ref, cast2_ref,
                  wg_ref, wu_ref, wd_ref, win_ref, ubuf, pbuf, *, tiles_per_seq, n_tiles):
    step = pl.program_id(0)
    tile = step - WEIGHT_STEPS

    @pl.when(step < WEIGHT_STEPS)
    def _():
        for src, dst in ((wg32_ref, wg_ref), (wu32_ref, wu_ref), (wd32_ref, wd_ref),
                         (win32_ref, win_ref)):
            rows = src.shape[0]
            first = pl.multiple_of(step * rows, BF16_SUBLANES)
            dst[pl.ds(first, rows), :] = src[...].astype(BF16)

    fold = functools.partial(jnp.dot, preferred_element_type=F32, precision=lax.Precision.HIGHEST)
    n_groups = len(POOL_WINDOWS)
    assert WEIGHT_STEPS == D_CONV // POOL_GROUP + n_groups
    for k in range(WEIGHT_STEPS):
        @pl.when(step == k)
        def _(k=k):
            rows = slice(k * POOL_GROUP, (k + 1) * POOL_GROUP)
            if k < D_CONV // POOL_GROUP:
                lhs, rhs = pw_ref[rows, :], wout_ref[0:D_CONV, :]
            else:
                g = k - D_CONV // POOL_GROUP
                lhs = poolw_ref[g] * pscale_ref[:, g * POOL_GROUP:(g + 1) * POOL_GROUP]
                rhs = wout_ref[rows, :]
            wmix_ref[rows, :] = fold(lhs, rhs).astype(BF16)

    @pl.when(step == 0)
    def _():
        ubuf[...] = jnp.zeros(ubuf.shape, F32)
        pbuf[...] = jnp.zeros(pbuf.shape, F32)

    step_refs = (x_ref, n1_ref, wg_ref, wu_ref, wd_ref, nm_ref, win_ref, dw_ref, dwb_ref,
                 lng_ref, lnb_ref, late0_ref, late1_ref, late2_ref, x1_ref, actmix_ref,
                 cast0_ref, cast1_ref, cast2_ref, ubuf, pbuf)

    @pl.when((tile >= 0) & (tile < n_tiles))
    def _():
        _front_step(tile, *step_refs, tiles_per_seq=tiles_per_seq, last=False)

    @pl.when(tile == n_tiles)
    def _():
        _front_step(tile, *step_refs, tiles_per_seq=tiles_per_seq, last=True)


def _front_step(i, x_ref, n1_ref, wg_ref, wu_ref, wd_ref, nm_ref, win_ref, dw_ref, dwb_ref,
                lng_ref, lnb_ref, late0_ref, late1_ref, late2_ref, x1_ref, actmix_ref,
                cast0_ref, cast1_ref, cast2_ref, ubuf, pbuf, *, tiles_per_seq, last):
    act_ref, mix_ref = actmix_ref.at[:, 0:D_CONV], actmix_ref.at[:, D_CONV:D_CONV + D_POOL]
    n_gate, n_stencil = wg_ref.shape[1] // MXU_COLS, TILE // STENCIL_ROWS
    n_early = n_stencil - STENCILS_UNDER_DOWN

    @pl.when(lax.rem(i, tiles_per_seq) == 1)
    def _():
        ubuf[:, 0:2 * CONV_HALO, :] = jnp.zeros((2, 2 * CONV_HALO, LANES), F32)
        pbuf[:, 0:2 * POOL_HALO, :] = jnp.zeros((2, 2 * POOL_HALO, LANES), F32)

    prev_first_pos = lax.rem(i + tiles_per_seq - 1, tiles_per_seq) * TILE
    stencil_chunk = functools.partial(
        _stencil_chunk, ubuf=ubuf, pbuf=pbuf, act_ref=act_ref, mix_ref=mix_ref, dw_ref=dw_ref,
        dwb_ref=dwb_ref, lng_ref=lng_ref, lnb_ref=lnb_ref, first_pos=prev_first_pos)
    if last:
        prev = None
        for r in range(n_stencil):
            _, prev = stencil_chunk(r, anchor=prev)
        return

    for src, dst in ((late0_ref, cast0_ref), (late1_ref, cast1_ref), (late2_ref, cast2_ref)):
        dst[...] = src[...].astype(BF16)

    x = x_ref[...]
    h = _rmsnorm(x, n1_ref[...]).astype(BF16)
    hid, prev = [], None
    for c in range(n_gate):
        cols = slice(c * MXU_COLS, (c + 1) * MXU_COLS)
        gate = jnp.dot(h, wg_ref[:, cols], preferred_element_type=F32)
        up = jnp.dot(h, wu_ref[:, cols], preferred_element_type=F32)
        act = gate * jax.nn.sigmoid(gate) * up
        for r in range(c * n_early // n_gate, (c + 1) * n_early // n_gate):
            done, prev = stencil_chunk(r, anchor=prev)
            act = _ordered_after(act, done)
        hid.append(act.astype(BF16))
        prev = _ready_token([gate[:SUBLANES, :LANES]] + ([] if prev is None else [prev]))
    hid = jnp.concatenate(hid, axis=1)
    x1 = x + FFN_RES_WEIGHT * jnp.dot(hid, wd_ref[...], preferred_element_type=F32)
    for r in range(n_early, n_stencil):
        done, prev = stencil_chunk(r, anchor=prev)
        x1 = _ordered_after(x1, done)
    x1_ref[...] = x1

    h = _rmsnorm(x1, nm_ref[...]).astype(BF16)
    proj = jnp.dot(h, win_ref[...], preferred_element_type=F32)
    u = proj[:, :D_CONV] * jax.nn.sigmoid(proj[:, D_CONV:2 * D_CONV])
    p = proj[:, 2 * D_CONV:]

    ubuf[:, 0:2 * CONV_HALO, :] = ubuf[:, 2 * TILE:2 * (TILE + CONV_HALO), :]
    pbuf[:, 0:2 * POOL_HALO, :] = pbuf[:, 2 * TILE:2 * (TILE + POOL_HALO), :]
    for s in range(D_CONV // LANES):
        _slab_rows(ubuf, s, CONV_HALO, TILE)[...] = u[:, s * LANES:(s + 1) * LANES]
    for s in range(D_POOL // LANES):
        _slab_rows(pbuf, s, POOL_HALO, TILE)[...] = p[:, s * LANES:(s + 1) * LANES]


def _front_call(x2d, n1, wg, wu, wd, nm, win, dw, dwb, lng, lnb, pw, poolw, pscale, wout,
                late_weights, *, tiles_per_seq):
    n, d = x2d.shape
    ff = wg.shape[1]
    nt = n // TILE
    resident = pl.Buffered(1)
    const = lambda s: (0, 0)
    cur = lambda s: (jnp.clip(s - WEIGHT_STEPS, 0, nt - 1), 0)
    prev = lambda s: (jnp.clip(s - WEIGHT_STEPS - 1, 0, nt - 1), 0)

    def own_block(w):
        assert w.shape[0] % (WEIGHT_STEPS * BF16_SUBLANES) == 0, w.shape
        return pl.BlockSpec((w.shape[0] // WEIGHT_STEPS, w.shape[1]),
                            lambda s: (jnp.minimum(s, WEIGHT_STEPS - 1), 0))

    def late_block(w):
        rows = next(r for r in range(BF16_SUBLANES, w.shape[0] + 1, BF16_SUBLANES)
                    if w.shape[0] % r == 0 and w.shape[0] // r <= nt)
        return pl.BlockSpec(
            (rows, w.shape[1]),
            lambda s: (jnp.clip(s - WEIGHT_STEPS, 0, w.shape[0] // rows - 1), 0))

    return pl.pallas_call(
        functools.partial(_front_kernel, tiles_per_seq=tiles_per_seq, n_tiles=nt),
        grid=(WEIGHT_STEPS + nt + 1,),
        in_specs=[
            pl.BlockSpec((TILE, d), cur),
            pl.BlockSpec((1, d), const),
            own_block(wg),
            own_block(wu),
            own_block(wd),
            pl.BlockSpec((1, d), const),
            own_block(win),
            pl.BlockSpec((CONV_WIDTH, D_CONV), const),
            pl.BlockSpec((1, D_CONV), const),
            pl.BlockSpec((1, D_CONV), const),
            pl.BlockSpec((1, D_CONV), const),
            pl.BlockSpec((D_CONV, D_CONV), const, pipeline_mode=resident),
            pl.BlockSpec((len(POOL_WINDOWS), POOL_GROUP, POOL_GROUP), lambda s: (0, 0, 0),
                         pipeline_mode=resident),
            pl.BlockSpec((1, D_POOL), const),
            pl.BlockSpec((D_CONV + D_POOL, d), const, pipeline_mode=resident),
        ] + [late_block(w) for w in late_weights],
        out_specs=[
            pl.BlockSpec((TILE, d), cur),
            pl.BlockSpec((TILE, D_CONV + D_POOL), prev),
            pl.BlockSpec((D_CONV + D_POOL, d), const),
        ] + [late_block(w) for w in late_weights],
        out_shape=[
            jax.ShapeDtypeStruct((n, d), F32),
            jax.ShapeDtypeStruct((n, D_CONV + D_POOL), BF16),
            jax.ShapeDtypeStruct((D_CONV + D_POOL, d), BF16),
        ] + [jax.ShapeDtypeStruct(w.shape, BF16) for w in late_weights],
        scratch_shapes=[
            pltpu.VMEM(wg.shape, BF16),
            pltpu.VMEM(wu.shape, BF16),
            pltpu.VMEM(wd.shape, BF16),
            pltpu.VMEM(win.shape, BF16),
            pltpu.VMEM((D_CONV // (2 * LANES), 2 * (CONV_HALO + TILE), LANES), F32),
            pltpu.VMEM((D_POOL // (2 * LANES), 2 * (POOL_HALO + TILE), LANES), F32),
        ],
        compiler_params=pltpu.CompilerParams(
            dimension_semantics=("arbitrary",),
            vmem_limit_bytes=V7X_VMEM_LIMIT_BYTES),
        name="front",
    )(x2d, n1, wg, wu, wd, nm, win, dw, dwb, lng, lnb, pw, poolw, pscale, wout, *late_weights)


def _back_kernel(x1_ref, actmix_ref, wmix_ref, n2_ref, wg_ref, wu_ref, wd_ref, nf_ref, y_ref):
    half = BACK_TILE // 2
    top, bottom = slice(0, half), slice(half, BACK_TILE)
    n_gate = wg_ref.shape[1] // MXU_COLS

    x2 = [x1_ref[rows, :] + jnp.dot(actmix_ref[rows, :], wmix_ref[...],
                                    preferred_element_type=F32) for rows in (top, bottom)]
    h = jnp.concatenate([_rmsnorm(v, n2_ref[...]).astype(BF16) for v in x2], axis=0)
    act = []
    for c in range(n_gate):
        cols = slice(c * MXU_COLS, (c + 1) * MXU_COLS)
        gate = jnp.dot(h, wg_ref[:, cols], preferred_element_type=F32)
        up = jnp.dot(h, wu_ref[:, cols], preferred_element_type=F32)
        act.append(gate * jax.nn.sigmoid(gate) * up)

    def finish(x2_half, hid_half):
        y = x2_half + FFN_RES_WEIGHT * jnp.dot(hid_half, wd_ref[...], preferred_element_type=F32)
        return _rmsnorm(y, nf_ref[...])

    y_top = finish(x2[0], jnp.concatenate([a[top].astype(BF16) for a in act], axis=1))
    y_ref[top, :] = y_top
    last = _ordered_after(act[-1][bottom], _ready_token([y_top]))
    hid_bottom = jnp.concatenate(
        [a[bottom].astype(BF16) for a in act[:-1]] + [last.astype(BF16)], axis=1)
    y_ref[bottom, :] = finish(x2[1], hid_bottom)


def _back_call(x1, actmix, wmix, n2, wg, wu, wd, nf):
    n, d = x1.shape
    ff = wg.shape[1]
    resident = pl.Buffered(1)
    const = lambda i: (0, 0)
    tile = lambda i: (i, 0)
    return pl.pallas_call(
        _back_kernel,
        grid=(n // BACK_TILE,),
        in_specs=[
            pl.BlockSpec((BACK_TILE, d), tile),
            pl.BlockSpec((BACK_TILE, D_CONV + D_POOL), tile),
            pl.BlockSpec((D_CONV + D_POOL, d), const, pipeline_mode=resident),
            pl.BlockSpec((1, d), const),
            pl.BlockSpec((d, ff), const, pipeline_mode=resident),
            pl.BlockSpec((d, ff), const, pipeline_mode=resident),
            pl.BlockSpec((ff, d), const, pipeline_mode=resident),
            pl.BlockSpec((1, d), const),
        ],
        out_specs=pl.BlockSpec((BACK_TILE, d), tile),
        out_shape=jax.ShapeDtypeStruct((n, d), F32),
        compiler_params=pltpu.CompilerParams(
            dimension_semantics=("arbitrary",),
            vmem_limit_bytes=V7X_VMEM_LIMIT_BYTES),
        name="back",
    )(x1, actmix, wmix, n2, wg, wu, wd, nf)


def kernel(x, ffn1_norm, ffn1_w_gate, ffn1_w_up, ffn1_w_down, mix_norm, w_in, conv_dw, conv_dw_b, conv_ln_g, conv_ln_b, conv_pw, pool_w, pool_scale, w_out, ffn2_norm, ffn2_w_gate, ffn2_w_up, ffn2_w_down, final_norm):
    b, t, d = x.shape
    assert t % TILE == 0 and t // TILE >= 2
    row = lambda v: v.reshape(1, -1)

    late = [ffn2_w_gate, ffn2_w_up, ffn2_w_down]
    x1, actmix, wmix, wg2, wu2, wd2 = _front_call(
        x.reshape(b * t, d), row(ffn1_norm), ffn1_w_gate, ffn1_w_up, ffn1_w_down,
        row(mix_norm), w_in, conv_dw, row(conv_dw_b), row(conv_ln_g), row(conv_ln_b),
        conv_pw, pool_w, row(pool_scale), w_out, late, tiles_per_seq=t // TILE)
    y = _back_call(x1, actmix, wmix, row(ffn2_norm), wg2, wu2, wd2, row(final_norm))
    return y.reshape(b, t, d)
```

```python
import functools

import jax
import jax.numpy as jnp
from jax import lax
from jax.experimental import pallas as pl
from jax.experimental.pallas import tpu as pltpu

D_CONV = 512
D_POOL = 512
CONV_WIDTH = 31
POOL_WINDOWS = (2, 4, 8, 16)
POOL_GROUP = D_POOL // len(POOL_WINDOWS)
RMS_EPS = 1e-6
LN_EPS = 1e-5
FFN_RES_WEIGHT = 0.5

V7X_VMEM_LIMIT_BYTES = 60 * 1024 * 1024
SUBLANES = 8
BF16_SUBLANES = 16
LANES = 128

TILE = 512
BACK_TILE = 1024
CONV_HALO = 32
POOL_HALO = 16
STENCIL_ROWS = 16
MXU_COLS = 256
STENCILS_UNDER_DOWN = 14
WEIGHT_STEPS = 8

BF16 = jnp.bfloat16
F32 = jnp.float32


def _rmsnorm(x, g):
    r = lax.rsqrt(jnp.mean(x * x, axis=-1, keepdims=True) + RMS_EPS)
    return (x * r) * g


def _slab_rows(ref, slab, first, rows):
    return ref.at[slab // 2, pl.ds(2 * first + slab % 2, rows, stride=2), :]


def _ready_token(arrays):
    tiles = [a[i:i + SUBLANES, j:j + LANES]
             for a in arrays
             for i in range(0, a.shape[0], SUBLANES)
             for j in range(0, a.shape[1], LANES)]
    return functools.reduce(jnp.maximum, tiles)


def _ordered_after(x, token, every_tile=False):
    if token is None:
        return x
    as_int = token.astype(jnp.int32)
    zero = lax.shift_right_logical(lax.shift_right_logical(as_int, jnp.int32(16)), jnp.int32(16))
    zero = zero.astype(F32)
    if every_tile:
        return x + jnp.concatenate([zero] * (x.shape[0] // SUBLANES), axis=0)
    head = x[:SUBLANES, :LANES] + zero
    if x.shape[1] > LANES:
        head = jnp.concatenate([head, x[:SUBLANES, LANES:]], axis=1)
    if x.shape[0] > SUBLANES:
        head = jnp.concatenate([head, x[SUBLANES:]], axis=0)
    return head


def _stencil_chunk(r, ubuf, pbuf, act_ref, mix_ref, dw_ref, dwb_ref, lng_ref, lnb_ref,
                   first_pos, anchor):
    n_slabs = D_CONV // LANES
    base = r * STENCIL_ROWS
    conv, conv_done = [], anchor
    for s in range(n_slabs):
        lanes = slice(s * LANES, (s + 1) * LANES)
        acc = _ordered_after(jnp.broadcast_to(dwb_ref[:, lanes], (STENCIL_ROWS, LANES)),
                             conv_done, every_tile=True)
        for k in range(CONV_WIDTH):
            first = base + CONV_HALO - (CONV_WIDTH - 1) + k
            acc = acc + _slab_rows(ubuf, s, first, STENCIL_ROWS)[...] * dw_ref[k:k + 1, lanes]
        conv.append(acc)
        conv_done = _ready_token([acc])
    mu = jnp.sum(sum(conv), axis=-1, keepdims=True) * (1.0 / D_CONV)
    cen = [c - mu for c in conv]
    var = jnp.sum(sum(c * c for c in cen), axis=-1, keepdims=True) * (1.0 / D_CONV)
    inv = lax.rsqrt(var + LN_EPS)
    results = []
    for s in range(n_slabs):
        lanes = slice(s * LANES, (s + 1) * LANES)
        ln = (cen[s] * inv) * lng_ref[:, lanes] + lnb_ref[:, lanes]
        act = ln * jax.nn.sigmoid(ln)
        act_ref[base:base + STENCIL_ROWS, lanes] = act.astype(BF16)
        results.append(act)
    pos1 = (first_pos + base + 1
            + lax.broadcasted_iota(jnp.int32, (STENCIL_ROWS, 1), 0)).astype(F32)
    for gi, w in enumerate(POOL_WINDOWS):
        cur = _ordered_after(_slab_rows(pbuf, gi, POOL_HALO + base, STENCIL_ROWS)[...], anchor)
        tot = cur
        for j in range(1, w):
            tot = tot + _slab_rows(pbuf, gi, POOL_HALO + base - j, STENCIL_ROWS)[...]
        mixed = tot * (1.0 / jnp.minimum(pos1, float(w))) - cur
        mix_ref[base:base + STENCIL_ROWS, gi * POOL_GROUP:(gi + 1) * POOL_GROUP] = (
            mixed.astype(BF16))
        results.append(mixed)
    return _ready_token(results), conv_done


def _front_kernel(x_ref, n1_ref, wg32_ref, wu32_ref, wd32_ref, nm_ref, win32_ref, dw_ref,
                  dwb_ref, lng_ref, lnb_ref, pw_ref, poolw_ref, pscale_ref, wout_ref,
                  late0_ref, late1_ref, late2_ref,
                  x1_ref, actmix_ref, wmix_ref, cast0_ref, cast1_ref, cast2_ref,
                  wg_ref, wu_ref, wd_ref, win_ref, ubuf, pbuf, *, tiles_per_seq):
    step = pl.program_id(0)

    @pl.when(step < WEIGHT_STEPS)
    def _():
        for src, dst in ((wg32_ref, wg_ref), (wu32_ref, wu_ref), (wd32_ref, wd_ref),
                         (win32_ref, win_ref)):
            rows = src.shape[0]
            first = pl.multiple_of(step * rows, BF16_SUBLANES)
            dst[pl.ds(first, rows), :] = src[...].astype(BF16)

    @pl.when(step >= WEIGHT_STEPS)
    def _():
        _front_step(step - WEIGHT_STEPS, x_ref, n1_ref, wg_ref, wu_ref, wd_ref, nm_ref, win_ref,
                    dw_ref, dwb_ref, lng_ref, lnb_ref, pw_ref, poolw_ref, pscale_ref, wout_ref,
                    late0_ref, late1_ref, late2_ref, x1_ref, actmix_ref, wmix_ref, cast0_ref,
                    cast1_ref, cast2_ref, ubuf, pbuf, tiles_per_seq=tiles_per_seq)


def _front_step(i, x_ref, n1_ref, wg_ref, wu_ref, wd_ref, nm_ref, win_ref, dw_ref, dwb_ref,
                lng_ref, lnb_ref, pw_ref, poolw_ref, pscale_ref, wout_ref,
                late0_ref, late1_ref, late2_ref,
                x1_ref, actmix_ref, wmix_ref, cast0_ref, cast1_ref, cast2_ref,
                ubuf, pbuf, *, tiles_per_seq):
    act_ref, mix_ref = actmix_ref.at[:, 0:D_CONV], actmix_ref.at[:, D_CONV:D_CONV + D_POOL]

    for src, dst in ((late0_ref, cast0_ref), (late1_ref, cast1_ref), (late2_ref, cast2_ref)):
        dst[...] = src[...].astype(BF16)
    n_gate, n_stencil = wg_ref.shape[1] // MXU_COLS, TILE // STENCIL_ROWS
    n_early = n_stencil - STENCILS_UNDER_DOWN

    @pl.when(i == 0)
    def _():
        ubuf[...] = jnp.zeros(ubuf.shape, F32)
        pbuf[...] = jnp.zeros(pbuf.shape, F32)
        fold = functools.partial(jnp.dot, preferred_element_type=F32,
                                 precision=lax.Precision.HIGHEST)
        wmix_ref[0:D_CONV, :] = fold(pw_ref[...], wout_ref[0:D_CONV, :]).astype(BF16)
        for gi in range(len(POOL_WINDOWS)):
            lanes = slice(gi * POOL_GROUP, (gi + 1) * POOL_GROUP)
            rows = slice(D_CONV + gi * POOL_GROUP, D_CONV + (gi + 1) * POOL_GROUP)
            wmix_ref[rows, :] = fold(poolw_ref[gi] * pscale_ref[:, lanes],
                                     wout_ref[rows, :]).astype(BF16)

    @pl.when(lax.rem(i, tiles_per_seq) == 1)
    def _():
        ubuf[:, 0:2 * CONV_HALO, :] = jnp.zeros((2, 2 * CONV_HALO, LANES), F32)
        pbuf[:, 0:2 * POOL_HALO, :] = jnp.zeros((2, 2 * POOL_HALO, LANES), F32)

    prev_first_pos = lax.rem(i + tiles_per_seq - 1, tiles_per_seq) * TILE
    stencil_chunk = functools.partial(
        _stencil_chunk, ubuf=ubuf, pbuf=pbuf, act_ref=act_ref, mix_ref=mix_ref, dw_ref=dw_ref,
        dwb_ref=dwb_ref, lng_ref=lng_ref, lnb_ref=lnb_ref, first_pos=prev_first_pos)

    x = x_ref[...]
    h = _rmsnorm(x, n1_ref[...]).astype(BF16)
    hid, prev = [], None
    for c in range(n_gate):
        cols = slice(c * MXU_COLS, (c + 1) * MXU_COLS)
        gate = jnp.dot(h, wg_ref[:, cols], preferred_element_type=F32)
        up = jnp.dot(h, wu_ref[:, cols], preferred_element_type=F32)
        act = gate * jax.nn.sigmoid(gate) * up
        for r in range(c * n_early // n_gate, (c + 1) * n_early // n_gate):
            done, prev = stencil_chunk(r, anchor=prev)
            act = _ordered_after(act, done)
        hid.append(act.astype(BF16))
        prev = _ready_token([gate[:SUBLANES, :LANES]] + ([] if prev is None else [prev]))
    hid = jnp.concatenate(hid, axis=1)
    x1 = x + FFN_RES_WEIGHT * jnp.dot(hid, wd_ref[...], preferred_element_type=F32)
    for r in range(n_early, n_stencil):
        done, prev = stencil_chunk(r, anchor=prev)
        x1 = _ordered_after(x1, done)
    x1_ref[...] = x1

    h = _rmsnorm(x1, nm_ref[...]).astype(BF16)
    proj = jnp.dot(h, win_ref[...], preferred_element_type=F32)
    u = proj[:, :D_CONV] * jax.nn.sigmoid(proj[:, D_CONV:2 * D_CONV])
    p = proj[:, 2 * D_CONV:]

    ubuf[:, 0:2 * CONV_HALO, :] = ubuf[:, 2 * TILE:2 * (TILE + CONV_HALO), :]
    pbuf[:, 0:2 * POOL_HALO, :] = pbuf[:, 2 * TILE:2 * (TILE + POOL_HALO), :]
    for s in range(D_CONV // LANES):
        _slab_rows(ubuf, s, CONV_HALO, TILE)[...] = u[:, s * LANES:(s + 1) * LANES]
    for s in range(D_POOL // LANES):
        _slab_rows(pbuf, s, POOL_HALO, TILE)[...] = p[:, s * LANES:(s + 1) * LANES]


def _front_call(x2d, n1, wg, wu, wd, nm, win, dw, dwb, lng, lnb, pw, poolw, pscale, wout,
                late_weights, *, tiles_per_seq):
    n, d = x2d.shape
    nt = n // TILE
    resident = pl.Buffered(1)
    const = lambda s: (0, 0)
    cur = lambda s: (jnp.clip(s - WEIGHT_STEPS, 0, nt - 1), 0)
    prev = lambda s: (jnp.clip(s - WEIGHT_STEPS - 1, 0, nt - 1), 0)

    def own_block(w):
        assert w.shape[0] % (WEIGHT_STEPS * BF16_SUBLANES) == 0, w.shape
        return pl.BlockSpec((w.shape[0] // WEIGHT_STEPS, w.shape[1]),
                            lambda s: (jnp.minimum(s, WEIGHT_STEPS - 1), 0))

    def late_block(w):
        rows = next(r for r in range(BF16_SUBLANES, w.shape[0] + 1, BF16_SUBLANES)
                    if w.shape[0] % r == 0 and w.shape[0] // r <= nt)
        return pl.BlockSpec(
            (rows, w.shape[1]),
            lambda s: (jnp.clip(s - WEIGHT_STEPS, 0, w.shape[0] // rows - 1), 0))

    return pl.pallas_call(
        functools.partial(_front_kernel, tiles_per_seq=tiles_per_seq),
        grid=(WEIGHT_STEPS + nt + 1,),
        in_specs=[
            pl.BlockSpec((TILE, d), cur),
            pl.BlockSpec((1, d), const),
            own_block(wg),
            own_block(wu),
            own_block(wd),
            pl.BlockSpec((1, d), const),
            own_block(win),
            pl.BlockSpec((CONV_WIDTH, D_CONV), const),
            pl.BlockSpec((1, D_CONV), const),
            pl.BlockSpec((1, D_CONV), const),
            pl.BlockSpec((1, D_CONV), const),
            pl.BlockSpec((D_CONV, D_CONV), const, pipeline_mode=resident),
            pl.BlockSpec((len(POOL_WINDOWS), POOL_GROUP, POOL_GROUP), lambda s: (0, 0, 0),
                         pipeline_mode=resident),
            pl.BlockSpec((1, D_POOL), const),
            pl.BlockSpec((D_CONV + D_POOL, d), const, pipeline_mode=resident),
        ] + [late_block(w) for w in late_weights],
        out_specs=[
            pl.BlockSpec((TILE, d), cur),
            pl.BlockSpec((TILE, D_CONV + D_POOL), prev),
            pl.BlockSpec((D_CONV + D_POOL, d), const),
        ] + [late_block(w) for w in late_weights],
        out_shape=[
            jax.ShapeDtypeStruct((n, d), F32),
            jax.ShapeDtypeStruct((n, D_CONV + D_POOL), BF16),
            jax.ShapeDtypeStruct((D_CONV + D_POOL, d), BF16),
        ] + [jax.ShapeDtypeStruct(w.shape, BF16) for w in late_weights],
        scratch_shapes=[
            pltpu.VMEM(wg.shape, BF16),
            pltpu.VMEM(wu.shape, BF16),
            pltpu.VMEM(wd.shape, BF16),
            pltpu.VMEM(win.shape, BF16),
            pltpu.VMEM((D_CONV // (2 * LANES), 2 * (CONV_HALO + TILE), LANES), F32),
            pltpu.VMEM((D_POOL // (2 * LANES), 2 * (POOL_HALO + TILE), LANES), F32),
        ],
        compiler_params=pltpu.CompilerParams(
            dimension_semantics=("arbitrary",),
            vmem_limit_bytes=V7X_VMEM_LIMIT_BYTES),
        name="front",
    )(x2d, n1, wg, wu, wd, nm, win, dw, dwb, lng, lnb, pw, poolw, pscale, wout, *late_weights)


def _back_kernel(x1_ref, actmix_ref, wmix_ref, n2_ref, wg_ref, wu_ref, wd_ref, nf_ref, y_ref):
    half = BACK_TILE // 2
    top, bottom = slice(0, half), slice(half, BACK_TILE)
    n_gate = wg_ref.shape[1] // MXU_COLS

    x2 = [x1_ref[rows, :] + jnp.dot(actmix_ref[rows, :], wmix_ref[...],
                                    preferred_element_type=F32) for rows in (top, bottom)]
    h = jnp.concatenate([_rmsnorm(v, n2_ref[...]).astype(BF16) for v in x2], axis=0)
    act = []
    for c in range(n_gate):
        cols = slice(c * MXU_COLS, (c + 1) * MXU_COLS)
        gate = jnp.dot(h, wg_ref[:, cols], preferred_element_type=F32)
        up = jnp.dot(h, wu_ref[:, cols], preferred_element_type=F32)
        act.append(gate * jax.nn.sigmoid(gate) * up)

    def finish(x2_half, hid_half):
        y = x2_half + FFN_RES_WEIGHT * jnp.dot(hid_half, wd_ref[...], preferred_element_type=F32)
        return _rmsnorm(y, nf_ref[...])

    y_top = finish(x2[0], jnp.concatenate([a[top].astype(BF16) for a in act], axis=1))
    y_ref[top, :] = y_top
    last = _ordered_after(act[-1][bottom], _ready_token([y_top]))
    hid_bottom = jnp.concatenate(
        [a[bottom].astype(BF16) for a in act[:-1]] + [last.astype(BF16)], axis=1)
    y_ref[bottom, :] = finish(x2[1], hid_bottom)


def _back_call(x1, actmix, wmix, n2, wg, wu, wd, nf):
    n, d = x1.shape
    ff = wg.shape[1]
    resident = pl.Buffered(1)
    const = lambda i: (0, 0)
    tile = lambda i: (i, 0)
    return pl.pallas_call(
        _back_kernel,
        grid=(n // BACK_TILE,),
        in_specs=[
            pl.BlockSpec((BACK_TILE, d), tile),
            pl.BlockSpec((BACK_TILE, D_CONV + D_POOL), tile),
            pl.BlockSpec((D_CONV + D_POOL, d), const, pipeline_mode=resident),
            pl.BlockSpec((1, d), const),
            pl.BlockSpec((d, ff), const, pipeline_mode=resident),
            pl.BlockSpec((d, ff), const, pipeline_mode=resident),
            pl.BlockSpec((ff, d), const, pipeline_mode=resident),
            pl.BlockSpec((1, d), const),
        ],
        out_specs=pl.BlockSpec((BACK_TILE, d), tile),
        out_shape=jax.ShapeDtypeStruct((n, d), F32),
        compiler_params=pltpu.CompilerParams(
            dimension_semantics=("arbitrary",),
            vmem_limit_bytes=V7X_VMEM_LIMIT_BYTES),
        name="back",
    )(x1, actmix, wmix, n2, wg, wu, wd, nf)


def kernel(x, ffn1_norm, ffn1_w_gate, ffn1_w_up, ffn1_w_down, mix_norm, w_in, conv_dw, conv_dw_b, conv_ln_g, conv_ln_b, conv_pw, pool_w, pool_scale, w_out, ffn2_norm, ffn2_w_gate, ffn2_w_up, ffn2_w_down, final_norm):
    b, t, d = x.shape
    assert t % TILE == 0 and t // TILE >= 2
    row = lambda v: v.reshape(1, -1)

    late = [ffn2_w_gate, ffn2_w_up, ffn2_w_down]
    x1, actmix, wmix, wg2, wu2, wd2 = _front_call(
        x.reshape(b * t, d), row(ffn1_norm), ffn1_w_gate, ffn1_w_up, ffn1_w_down,
        row(mix_norm), w_in, conv_dw, row(conv_dw_b), row(conv_ln_g), row(conv_ln_b),
        conv_pw, pool_w, row(pool_scale), w_out, late, tiles_per_seq=t // TILE)
    y = _back_call(x1, actmix, wmix, row(ffn2_norm), wg2, wu2, wd2, row(final_norm))
    return y.reshape(b, t, d)
```

```python
import functools

import jax
import jax.numpy as jnp
from jax import lax
from jax.experimental import pallas as pl
from jax.experimental.pallas import tpu as pltpu

D_CONV = 512
D_POOL = 512
CONV_WIDTH = 31
POOL_WINDOWS = (2, 4, 8, 16)
POOL_GROUP = D_POOL // len(POOL_WINDOWS)
D_IN = 2 * D_CONV + D_POOL
RMS_EPS = 1e-6
LN_EPS = 1e-5
FFN_RES_WEIGHT = 0.5

V7X_VMEM_LIMIT_BYTES = 60 * 1024 * 1024
SUBLANES = 8
BF16_SUBLANES = 16
LANES = 128

TILE = 512
BACK_TILE = 1024
CONV_HALO = 32
POOL_HALO = 16
STENCIL_ROWS = 16
MXU_COLS = 256
STENCILS_UNDER_DOWN = 16
WEIGHT_STEPS = 8

BF16 = jnp.bfloat16
F32 = jnp.float32


def _rmsnorm(x, g):
    r = lax.rsqrt(jnp.mean(x * x, axis=-1, keepdims=True) + RMS_EPS)
    return (x * r) * g


def _swiglu_residual(x, ng_ref, wg_ref, wu_ref, wd_ref):
    h = _rmsnorm(x, ng_ref[...]).astype(BF16)
    gate = jnp.dot(h, wg_ref[...], preferred_element_type=F32)
    up = jnp.dot(h, wu_ref[...], preferred_element_type=F32)
    hid = (gate * jax.nn.sigmoid(gate) * up).astype(BF16)
    return x + FFN_RES_WEIGHT * jnp.dot(hid, wd_ref[...], preferred_element_type=F32)


def _slab_rows(ref, slab, first, rows):
    return ref.at[slab // 2, pl.ds(2 * first + slab % 2, rows, stride=2), :]


def _ready_token(arrays):
    tiles = [a[i:i + SUBLANES, j:j + LANES]
             for a in arrays
             for i in range(0, a.shape[0], SUBLANES)
             for j in range(0, a.shape[1], LANES)]
    return functools.reduce(jnp.maximum, tiles)


def _ordered_after(x, token, every_tile=False):
    if token is None:
        return x
    as_int = token.astype(jnp.int32)
    zero = lax.shift_right_logical(lax.shift_right_logical(as_int, jnp.int32(16)), jnp.int32(16))
    zero = zero.astype(F32)
    if every_tile:
        return x + jnp.concatenate([zero] * (x.shape[0] // SUBLANES), axis=0)
    head = x[:SUBLANES, :LANES] + zero
    if x.shape[1] > LANES:
        head = jnp.concatenate([head, x[:SUBLANES, LANES:]], axis=1)
    if x.shape[0] > SUBLANES:
        head = jnp.concatenate([head, x[SUBLANES:]], axis=0)
    return head


def _stencil_chunk(r, ubuf, pbuf, act_ref, mix_ref, dw_ref, dwb_ref, lng_ref, lnb_ref,
                   first_pos, anchor):
    n_slabs = D_CONV // LANES
    base = r * STENCIL_ROWS
    conv, conv_done = [], anchor
    for s in range(n_slabs):
        lanes = slice(s * LANES, (s + 1) * LANES)
        acc = _ordered_after(jnp.broadcast_to(dwb_ref[:, lanes], (STENCIL_ROWS, LANES)),
                             conv_done, every_tile=True)
        for k in range(CONV_WIDTH):
            first = base + CONV_HALO - (CONV_WIDTH - 1) + k
            acc = acc + _slab_rows(ubuf, s, first, STENCIL_ROWS)[...] * dw_ref[k:k + 1, lanes]
        conv.append(acc)
        conv_done = _ready_token([acc])
    mu = jnp.sum(sum(conv), axis=-1, keepdims=True) * (1.0 / D_CONV)
    cen = [c - mu for c in conv]
    var = jnp.sum(sum(c * c for c in cen), axis=-1, keepdims=True) * (1.0 / D_CONV)
    inv = lax.rsqrt(var + LN_EPS)
    results = []
    for s in range(n_slabs):
        lanes = slice(s * LANES, (s + 1) * LANES)
        ln = (cen[s] * inv) * lng_ref[:, lanes] + lnb_ref[:, lanes]
        act = ln * jax.nn.sigmoid(ln)
        act_ref[base:base + STENCIL_ROWS, lanes] = act.astype(BF16)
        results.append(act)
    pos1 = (first_pos + base + 1
            + lax.broadcasted_iota(jnp.int32, (STENCIL_ROWS, 1), 0)).astype(F32)
    for gi, w in enumerate(POOL_WINDOWS):
        cur = _ordered_after(_slab_rows(pbuf, gi, POOL_HALO + base, STENCIL_ROWS)[...], anchor)
        tot = cur
        for j in range(1, w):
            tot = tot + _slab_rows(pbuf, gi, POOL_HALO + base - j, STENCIL_ROWS)[...]
        mixed = tot * (1.0 / jnp.minimum(pos1, float(w))) - cur
        mix_ref[base:base + STENCIL_ROWS, gi * POOL_GROUP:(gi + 1) * POOL_GROUP] = (
            mixed.astype(BF16))
        results.append(mixed)
    return _ready_token(results), conv_done


def _front_kernel(x_ref, n1_ref, wg32_ref, wu32_ref, wd32_ref, nm_ref, win32_ref, dw_ref,
                  dwb_ref, lng_ref, lnb_ref, pw_ref, poolw_ref, pscale_ref, wout_ref,
                  late0_ref, late1_ref, late2_ref,
                  x1_ref, actmix_ref, wmix_ref, cast0_ref, cast1_ref, cast2_ref,
                  wg_ref, wu_ref, wd_ref, win_ref, ubuf, pbuf, *, tiles_per_seq):
    step = pl.program_id(0)

    @pl.when(step < WEIGHT_STEPS)
    def _():
        for src, dst in ((wg32_ref, wg_ref), (wu32_ref, wu_ref), (wd32_ref, wd_ref),
                         (win32_ref, win_ref)):
            rows = src.shape[0]
            first = pl.multiple_of(step * rows, BF16_SUBLANES)
            dst[pl.ds(first, rows), :] = src[...].astype(BF16)

    @pl.when(step >= WEIGHT_STEPS)
    def _():
        _front_step(step - WEIGHT_STEPS, x_ref, n1_ref, wg_ref, wu_ref, wd_ref, nm_ref, win_ref,
                    dw_ref, dwb_ref, lng_ref, lnb_ref, pw_ref, poolw_ref, pscale_ref, wout_ref,
                    late0_ref, late1_ref, late2_ref, x1_ref, actmix_ref, wmix_ref, cast0_ref,
                    cast1_ref, cast2_ref, ubuf, pbuf, tiles_per_seq=tiles_per_seq)


def _front_step(i, x_ref, n1_ref, wg_ref, wu_ref, wd_ref, nm_ref, win_ref, dw_ref, dwb_ref,
                lng_ref, lnb_ref, pw_ref, poolw_ref, pscale_ref, wout_ref,
                late0_ref, late1_ref, late2_ref,
                x1_ref, actmix_ref, wmix_ref, cast0_ref, cast1_ref, cast2_ref,
                ubuf, pbuf, *, tiles_per_seq):
    act_ref, mix_ref = actmix_ref.at[:, 0:D_CONV], actmix_ref.at[:, D_CONV:D_CONV + D_POOL]

    for src, dst in ((late0_ref, cast0_ref), (late1_ref, cast1_ref), (late2_ref, cast2_ref)):
        dst[...] = src[...].astype(BF16)
    n_gate, n_stencil = wg_ref.shape[1] // MXU_COLS, TILE // STENCIL_ROWS
    n_early = n_stencil - STENCILS_UNDER_DOWN

    @pl.when(i == 0)
    def _():
        ubuf[...] = jnp.zeros(ubuf.shape, F32)
        pbuf[...] = jnp.zeros(pbuf.shape, F32)
        fold = functools.partial(jnp.dot, preferred_element_type=F32,
                                 precision=lax.Precision.HIGHEST)
        wmix_ref[0:D_CONV, :] = fold(pw_ref[...], wout_ref[0:D_CONV, :]).astype(BF16)
        for gi in range(len(POOL_WINDOWS)):
            lanes = slice(gi * POOL_GROUP, (gi + 1) * POOL_GROUP)
            rows = slice(D_CONV + gi * POOL_GROUP, D_CONV + (gi + 1) * POOL_GROUP)
            wmix_ref[rows, :] = fold(poolw_ref[gi] * pscale_ref[:, lanes],
                                     wout_ref[rows, :]).astype(BF16)

    @pl.when(lax.rem(i, tiles_per_seq) == 1)
    def _():
        ubuf[:, 0:2 * CONV_HALO, :] = jnp.zeros((2, 2 * CONV_HALO, LANES), F32)
        pbuf[:, 0:2 * POOL_HALO, :] = jnp.zeros((2, 2 * POOL_HALO, LANES), F32)

    prev_first_pos = lax.rem(i + tiles_per_seq - 1, tiles_per_seq) * TILE
    stencil_chunk = functools.partial(
        _stencil_chunk, ubuf=ubuf, pbuf=pbuf, act_ref=act_ref, mix_ref=mix_ref, dw_ref=dw_ref,
        dwb_ref=dwb_ref, lng_ref=lng_ref, lnb_ref=lnb_ref, first_pos=prev_first_pos)

    x = x_ref[...]
    h = _rmsnorm(x, n1_ref[...]).astype(BF16)
    hid, prev = [], None
    for c in range(n_gate):
        cols = slice(c * MXU_COLS, (c + 1) * MXU_COLS)
        gate = jnp.dot(h, wg_ref[:, cols], preferred_element_type=F32)
        up = jnp.dot(h, wu_ref[:, cols], preferred_element_type=F32)
        act = gate * jax.nn.sigmoid(gate) * up
        for r in range(c * n_early // n_gate, (c + 1) * n_early // n_gate):
            done, prev = stencil_chunk(r, anchor=prev)
            act = _ordered_after(act, done)
        hid.append(act.astype(BF16))
        prev = _ready_token([gate[:SUBLANES, :LANES]] + ([] if prev is None else [prev]))
    hid = jnp.concatenate(hid, axis=1)
    x1 = x + FFN_RES_WEIGHT * jnp.dot(hid, wd_ref[...], preferred_element_type=F32)
    for r in range(n_early, n_stencil):
        done, prev = stencil_chunk(r, anchor=prev)
        x1 = _ordered_after(x1, done)
    x1_ref[...] = x1

    h = _rmsnorm(x1, nm_ref[...]).astype(BF16)
    proj = jnp.dot(h, win_ref[...], preferred_element_type=F32)
    u = proj[:, :D_CONV] * jax.nn.sigmoid(proj[:, D_CONV:2 * D_CONV])
    p = proj[:, 2 * D_CONV:]

    ubuf[:, 0:2 * CONV_HALO, :] = ubuf[:, 2 * TILE:2 * (TILE + CONV_HALO), :]
    pbuf[:, 0:2 * POOL_HALO, :] = pbuf[:, 2 * TILE:2 * (TILE + POOL_HALO), :]
    for s in range(D_CONV // LANES):
        _slab_rows(ubuf, s, CONV_HALO, TILE)[...] = u[:, s * LANES:(s + 1) * LANES]
    for s in range(D_POOL // LANES):
        _slab_rows(pbuf, s, POOL_HALO, TILE)[...] = p[:, s * LANES:(s + 1) * LANES]


def _front_call(x2d, n1, wg, wu, wd, nm, win, dw, dwb, lng, lnb, pw, poolw, pscale, wout,
                late_weights, *, tiles_per_seq):
    n, d = x2d.shape
    ff = wg.shape[1]
    nt = n // TILE
    resident = pl.Buffered(1)
    const = lambda s: (0, 0)
    cur = lambda s: (jnp.clip(s - WEIGHT_STEPS, 0, nt - 1), 0)
    prev = lambda s: (jnp.clip(s - WEIGHT_STEPS - 1, 0, nt - 1), 0)

    def own_block(w):
        assert w.shape[0] % (WEIGHT_STEPS * BF16_SUBLANES) == 0, w.shape
        return pl.BlockSpec((w.shape[0] // WEIGHT_STEPS, w.shape[1]),
                            lambda s: (jnp.minimum(s, WEIGHT_STEPS - 1), 0))

    def late_block(w):
        rows = next(r for r in range(BF16_SUBLANES, w.shape[0] + 1, BF16_SUBLANES)
                    if w.shape[0] % r == 0 and w.shape[0] // r <= nt)
        return pl.BlockSpec(
            (rows, w.shape[1]),
            lambda s: (jnp.clip(s - WEIGHT_STEPS, 0, w.shape[0] // rows - 1), 0))

    return pl.pallas_call(
        functools.partial(_front_kernel, tiles_per_seq=tiles_per_seq),
        grid=(WEIGHT_STEPS + nt + 1,),
        in_specs=[
            pl.BlockSpec((TILE, d), cur),
            pl.BlockSpec((1, d), const),
            own_block(wg),
            own_block(wu),
            own_block(wd),
            pl.BlockSpec((1, d), const),
            own_block(win),
            pl.BlockSpec((CONV_WIDTH, D_CONV), const),
            pl.BlockSpec((1, D_CONV), const),
            pl.BlockSpec((1, D_CONV), const),
            pl.BlockSpec((1, D_CONV), const),
            pl.BlockSpec((D_CONV, D_CONV), const, pipeline_mode=resident),
            pl.BlockSpec((len(POOL_WINDOWS), POOL_GROUP, POOL_GROUP), lambda s: (0, 0, 0),
                         pipeline_mode=resident),
            pl.BlockSpec((1, D_POOL), const),
            pl.BlockSpec((D_CONV + D_POOL, d), const, pipeline_mode=resident),
        ] + [late_block(w) for w in late_weights],
        out_specs=[
            pl.BlockSpec((TILE, d), cur),
            pl.BlockSpec((TILE, D_CONV + D_POOL), prev),
            pl.BlockSpec((D_CONV + D_POOL, d), const),
        ] + [late_block(w) for w in late_weights],
        out_shape=[
            jax.ShapeDtypeStruct((n, d), F32),
            jax.ShapeDtypeStruct((n, D_CONV + D_POOL), BF16),
            jax.ShapeDtypeStruct((D_CONV + D_POOL, d), BF16),
        ] + [jax.ShapeDtypeStruct(w.shape, BF16) for w in late_weights],
        scratch_shapes=[
            pltpu.VMEM(wg.shape, BF16),
            pltpu.VMEM(wu.shape, BF16),
            pltpu.VMEM(wd.shape, BF16),
            pltpu.VMEM(win.shape, BF16),
            pltpu.VMEM((D_CONV // (2 * LANES), 2 * (CONV_HALO + TILE), LANES), F32),
            pltpu.VMEM((D_POOL // (2 * LANES), 2 * (POOL_HALO + TILE), LANES), F32),
        ],
        compiler_params=pltpu.CompilerParams(
            dimension_semantics=("arbitrary",),
            vmem_limit_bytes=V7X_VMEM_LIMIT_BYTES),
        name="front",
    )(x2d, n1, wg, wu, wd, nm, win, dw, dwb, lng, lnb, pw, poolw, pscale, wout, *late_weights)


def _back_kernel(x1_ref, actmix_ref, wmix_ref, n2_ref, wg_ref, wu_ref, wd_ref, nf_ref, y_ref):
    half = BACK_TILE // 2
    top, bottom = slice(0, half), slice(half, BACK_TILE)
    n_gate = wg_ref.shape[1] // MXU_COLS

    x2 = [x1_ref[rows, :] + jnp.dot(actmix_ref[rows, :], wmix_ref[...],
                                    preferred_element_type=F32) for rows in (top, bottom)]
    h = jnp.concatenate([_rmsnorm(v, n2_ref[...]).astype(BF16) for v in x2], axis=0)
    act = []
    for c in range(n_gate):
        cols = slice(c * MXU_COLS, (c + 1) * MXU_COLS)
        gate = jnp.dot(h, wg_ref[:, cols], preferred_element_type=F32)
        up = jnp.dot(h, wu_ref[:, cols], preferred_element_type=F32)
        act.append(gate * jax.nn.sigmoid(gate) * up)

    def finish(x2_half, hid_half):
        y = x2_half + FFN_RES_WEIGHT * jnp.dot(hid_half, wd_ref[...], preferred_element_type=F32)
        return _rmsnorm(y, nf_ref[...])

    y_top = finish(x2[0], jnp.concatenate([a[top].astype(BF16) for a in act], axis=1))
    y_ref[top, :] = y_top
    last = _ordered_after(act[-1][bottom], _ready_token([y_top]))
    hid_bottom = jnp.concatenate(
        [a[bottom].astype(BF16) for a in act[:-1]] + [last.astype(BF16)], axis=1)
    y_ref[bottom, :] = finish(x2[1], hid_bottom)


def _back_call(x1, actmix, wmix, n2, wg, wu, wd, nf):
    n, d = x1.shape
    ff = wg.shape[1]
    resident = pl.Buffered(1)
    const = lambda i: (0, 0)
    tile = lambda i: (i, 0)
    return pl.pallas_call(
        _back_kernel,
        grid=(n // BACK_TILE,),
        in_specs=[
            pl.BlockSpec((BACK_TILE, d), tile),
            pl.BlockSpec((BACK_TILE, D_CONV + D_POOL), tile),
            pl.BlockSpec((D_CONV + D_POOL, d), const, pipeline_mode=resident),
            pl.BlockSpec((1, d), const),
            pl.BlockSpec((d, ff), const, pipeline_mode=resident),
            pl.BlockSpec((d, ff), const, pipeline_mode=resident),
            pl.BlockSpec((ff, d), const, pipeline_mode=resident),
            pl.BlockSpec((1, d), const),
        ],
        out_specs=pl.BlockSpec((BACK_TILE, d), tile),
        out_shape=jax.ShapeDtypeStruct((n, d), F32),
        compiler_params=pltpu.CompilerParams(
            dimension_semantics=("arbitrary",),
            vmem_limit_bytes=V7X_VMEM_LIMIT_BYTES),
        name="back",
    )(x1, actmix, wmix, n2, wg, wu, wd, nf)


def kernel(x, ffn1_norm, ffn1_w_gate, ffn1_w_up, ffn1_w_down, mix_norm, w_in, conv_dw, conv_dw_b, conv_ln_g, conv_ln_b, conv_pw, pool_w, pool_scale, w_out, ffn2_norm, ffn2_w_gate, ffn2_w_up, ffn2_w_down, final_norm):
    b, t, d = x.shape
    assert t % TILE == 0 and t // TILE >= 2
    row = lambda v: v.reshape(1, -1)

    late = [ffn2_w_gate, ffn2_w_up, ffn2_w_down]
    x1, actmix, wmix, wg2, wu2, wd2 = _front_call(
        x.reshape(b * t, d), row(ffn1_norm), ffn1_w_gate, ffn1_w_up, ffn1_w_down,
        row(mix_norm), w_in, conv_dw, row(conv_dw_b), row(conv_ln_g), row(conv_ln_b),
        conv_pw, pool_w, row(pool_scale), w_out, late, tiles_per_seq=t // TILE)
    y = _back_call(x1, actmix, wmix, row(ffn2_norm), wg2, wu2, wd2, row(final_norm))
    return y.reshape(b, t, d)
```
